```python
import jax, jax.numpy as jnp
from jax import lax
import numpy as np

D_MODEL = 1024
BATCH = 8
SEQ = 4096
DEPTH = 1

CHUNK = 64
W_POOL = D_MODEL // 2
POOL_WINDOWS = (2, 4, 8, 16)
N_POOL_GROUPS = len(POOL_WINDOWS)
POOL_GROUP = W_POOL // N_POOL_GROUPS
W_CONV = D_MODEL
CONV_K = 31
D_FF = ((8 * D_MODEL // 3 + 127) // 128) * 128
FFN_CONV_K = 3
N_IN = W_POOL + 2 * W_CONV + 2 * D_MODEL
N_MOD = 6
EPS = 1e-6

kernel_name = "hybrid_pool_conformer_convffn_block"


def rmsnorm(x, g):
    xf = x.astype(jnp.float32)
    y = xf * lax.rsqrt(jnp.mean(xf * xf, axis=-1, keepdims=True) + EPS)
    return (y * g.astype(jnp.float32)).astype(x.dtype)


def layernorm(x, g, b):
    xf = x.astype(jnp.float32)
    mu = jnp.mean(xf, axis=-1, keepdims=True)
    xc = xf - mu
    y = xc * lax.rsqrt(jnp.mean(xc * xc, axis=-1, keepdims=True) + EPS)
    return (y * g.astype(jnp.float32) + b.astype(jnp.float32)).astype(x.dtype)


def modulate(h, shift, scale):
    return h * (1.0 + scale[:, None, :]) + shift[:, None, :]


def causal_dwconv(u, w, b):
    k, ch = w.shape
    y = lax.conv_general_dilated(
        u, w[:, None, :].astype(u.dtype), window_strides=(1,), padding=[(k - 1, 0)],
        dimension_numbers=("NWC", "WIO", "NWC"), feature_group_count=ch)
    return y + b


def trailing_mean_minus_self(u, window):
    s = u.shape[1]
    uf = u.astype(jnp.float32)
    cs = jnp.cumsum(uf, axis=1)
    prev = jnp.pad(cs, ((0, 0), (window, 0), (0, 0)))[:, :s]
    count = jnp.minimum(jnp.arange(1, s + 1), window).astype(jnp.float32)[None, :, None]
    return ((cs - prev) / count - uf).astype(u.dtype)


def setup_inputs(seed: int = 0) -> dict:
    key = jax.random.key(seed)
    ks = jax.random.split(key, 24)
    L, D = DEPTH, D_MODEL
    nrm = lambda k, shape, s: jax.random.normal(k, shape, jnp.float32) * s
    return {
        "x": nrm(ks[0], (BATCH, SEQ, D), 1.0),
        "c": nrm(ks[1], (BATCH, D), 1.0),
        "w_ada": nrm(ks[2], (L, D, N_MOD * D), 0.5 * D ** -0.5),
        "b_ada": nrm(ks[3], (L, N_MOD * D), 0.02),
        "g_norm1": 1.0 + nrm(ks[4], (L, D), 0.02),
        "w_in": nrm(ks[5], (L, D, N_IN), D ** -0.5),
        "w_pool_grp": nrm(ks[6], (L, N_POOL_GROUPS, POOL_GROUP, POOL_GROUP), POOL_GROUP ** -0.5),
        "b_pool_grp": nrm(ks[7], (L, N_POOL_GROUPS, POOL_GROUP), 0.02),
        "pool_scale": 1.0 + nrm(ks[8], (L, W_POOL), 0.02),
        "w_pool_out": nrm(ks[9], (L, W_POOL, D), W_POOL ** -0.5),
        "conv_w": nrm(ks[10], (L, CONV_K, W_CONV), CONV_K ** -0.5),
        "conv_b": nrm(ks[11], (L, W_CONV), 0.02),
        "ln_g": 1.0 + nrm(ks[12], (L, W_CONV), 0.02),
        "ln_b": nrm(ks[13], (L, W_CONV), 0.02),
        "w_conv_out": nrm(ks[14], (L, W_CONV, D), W_CONV ** -0.5),
        "w_out": nrm(ks[15], (L, D, D), D ** -0.5),
        "g_norm2": 1.0 + nrm(ks[16], (L, D), 0.02),
        "w_up": nrm(ks[17], (L, D, 2 * D_FF), D ** -0.5),
        "ffn_conv_w": nrm(ks[18], (L, FFN_CONV_K, 2 * D_FF), FFN_CONV_K ** -0.5),
        "ffn_conv_b": nrm(ks[19], (L, 2 * D_FF), 0.02),
        "w_down": nrm(ks[20], (L, D_FF, D), D_FF ** -0.5),
        "g_final": 1.0 + nrm(ks[21], (D,), 0.02),
    }


def reference(x, c, w_ada, b_ada, g_norm1, w_in, w_pool_grp, b_pool_grp, pool_scale,
              w_pool_out, conv_w, conv_b, ln_g, ln_b, w_conv_out, w_out, g_norm2,
              w_up, ffn_conv_w, ffn_conv_b, w_down, g_final):
    b, s, d = x.shape
    c_act = jax.nn.silu(c)
    for l in range(DEPTH):
        mod = jnp.einsum("bd,de->be", c_act, w_ada[l]) + b_ada[l]
        sh1, sc1, ga1, sh2, sc2, ga2 = jnp.split(mod, N_MOD, axis=-1)

        h = modulate(rmsnorm(x, g_norm1[l]), sh1, sc1)
        z = jnp.einsum("bsd,de->bse", h, w_in[l])
        z_pool, z_conv, z_ga, z_gb = jnp.split(
            z, [W_POOL, W_POOL + 2 * W_CONV, W_POOL + 2 * W_CONV + d], axis=-1)

        zp = z_pool.reshape(b, s, N_POOL_GROUPS, POOL_GROUP)
        pooled = jnp.stack(
            [trailing_mean_minus_self(zp[:, :, g], w) for g, w in enumerate(POOL_WINDOWS)],
            axis=2)
        pa = jnp.einsum("bsgc,gce->bsge", pooled, w_pool_grp[l]) + b_pool_grp[l]
        pa = pa.reshape(b, s, W_POOL) * pool_scale[l]
        y_a = jnp.einsum("bsc,cd->bsd", pa, w_pool_out[l])

        val, gt = jnp.split(z_conv, 2, axis=-1)
        u = val * jax.nn.sigmoid(gt)
        u = causal_dwconv(u, conv_w[l], conv_b[l])
        u = jax.nn.silu(layernorm(u, ln_g[l], ln_b[l]))
        y_b = jnp.einsum("bsc,cd->bsd", u, w_conv_out[l])

        merged = jax.nn.sigmoid(z_ga) * y_a + jax.nn.sigmoid(z_gb) * y_b
        x = x + ga1[:, None, :] * jnp.einsum("bsd,de->bse", merged, w_out[l])

        h = modulate(rmsnorm(x, g_norm2[l]), sh2, sc2)
        up = jnp.einsum("bsd,df->bsf", h, w_up[l])
        up = causal_dwconv(up, ffn_conv_w[l], ffn_conv_b[l])
        ug, uv = jnp.split(up, 2, axis=-1)
        f = jax.nn.silu(ug) * uv
        x = x + ga2[:, None, :] * jnp.einsum("bsf,fd->bsd", f, w_down[l])

    return rmsnorm(x, g_final)
```

```python
import functools

import jax
import jax.numpy as jnp
from jax import lax
from jax.experimental import pallas as pl
from jax.experimental.pallas import tpu as pltpu

F32 = jnp.float32
BF16 = jnp.bfloat16

EPS = 1e-6
POOL_WINDOWS = (2, 4, 8, 16)
N_MOD = 6

SUBLANES = 8
LANES = 128
TOKENS_PER_TILE = 512
ROW_CHUNK = 64
FFN_CHUNK = 256
ADA_COLS = 512
VMEM_LIMIT_BYTES = 60 * 1024 * 1024


def _dot(a, b):
    return jnp.dot(a, b, preferred_element_type=F32)


def _sigmoid(x):
    return jax.nn.sigmoid(x)


def _row_loop(n_rows, chunk, body):
    def step(i, carry):
        body(pl.multiple_of(i * chunk, chunk))
        return carry
    lax.fori_loop(0, n_rows // chunk, step, 0)


def _halo_groups(cur_tail, prev_tail):
    n = cur_tail.shape[0] // SUBLANES
    sub = lax.broadcasted_iota(jnp.int32, cur_tail.shape, 0) % SUBLANES
    mixed = jnp.where(sub == SUBLANES - 1, prev_tail, cur_tail)
    groups = [pltpu.roll(mixed[g * SUBLANES:(g + 1) * SUBLANES], 1, 0) for g in range(n)]
    return jnp.concatenate(groups, axis=0) if n > 1 else groups[0]


def _ada_kernel(c_ref, w_ref, b_ref, o_ref):
    c = c_ref[...]
    ca = c * _sigmoid(c)
    w = w_ref[...]
    ca_hi = ca.astype(BF16)
    ca_lo = (ca - ca_hi.astype(F32)).astype(BF16)
    w_hi = w.astype(BF16)
    w_lo = (w - w_hi.astype(F32)).astype(BF16)
    acc = _dot(ca_hi, w_hi) + _dot(ca_lo, w_hi) + _dot(ca_hi, w_lo)
    o_ref[...] = acc + b_ref[...]


def _ada_call(c, w_ada, b_ada):
    b, d = c.shape
    n = w_ada.shape[1]
    return pl.pallas_call(
        _ada_kernel,
        grid=(n // ADA_COLS,),
        in_specs=[
            pl.BlockSpec((b, d), lambda j: (0, 0)),
            pl.BlockSpec((d, ADA_COLS), lambda j: (0, j)),
            pl.BlockSpec((1, ADA_COLS), lambda j: (0, j)),
        ],
        out_specs=pl.BlockSpec((b, ADA_COLS), lambda j: (0, j)),
        out_shape=jax.ShapeDtypeStruct((b, n), F32),
        compiler_params=pltpu.CompilerParams(dimension_semantics=("arbitrary",)),
        name="adaln_mod",
    )(c, w_ada, b_ada.reshape(1, n))


def _mix_kernel(x_ref, mod_ref, g1_ref, win_ref, wgrp_ref, bgrp_ref, pscale_ref, wpo_ref,
                cw_ref, cb_ref, lng_ref, lnb_ref, wco_ref, wout_ref,
                o_ref,
                hbuf, zpbuf, ubuf, zga, zgb, invc, pbuf, ya, cbuf, act, zcarry, ucarry,
                *, d, w_pool, conv_k, t):
    i = pl.program_id(1)
    l = t // SUBLANES
    n_grp = len(POOL_WINDOWS)
    pg = w_pool // n_grp
    hp = max(POOL_WINDOWS)
    hu = conv_k - 1
    hp8, hu8 = hp * SUBLANES, hu * SUBLANES

    @pl.when(i == 0)
    def _():
        zcarry[...] = jnp.zeros_like(zcarry)
        ucarry[...] = jnp.zeros_like(ucarry)
        r = lax.broadcasted_iota(jnp.int32, (t, pg), 0)
        tok = (r % SUBLANES) * l + r // SUBLANES
        for g, w in enumerate(POOL_WINDOWS):
            cnt = jnp.minimum(tok + 1, w).astype(F32)
            invc[:, g * pg:(g + 1) * pg] = 1.0 / cnt

    @pl.when(i == 1)
    def _():
        for g, w in enumerate(POOL_WINDOWS):
            invc[:, g * pg:(g + 1) * pg] = jnp.full((t, pg), 1.0 / w, F32)

    sh1 = mod_ref[0:1, :]
    gain1 = g1_ref[...] * (1.0 + mod_ref[1:2, :])
    ga1 = mod_ref[2:3, :]

    for jp in range(l // 2):
        xa = x_ref[:, (2 * jp) * d:(2 * jp + 1) * d]
        xb = x_ref[:, (2 * jp + 1) * d:(2 * jp + 2) * d]
        xx = jnp.concatenate([xa, xb], axis=0)
        ms = jnp.mean(xx * xx, axis=-1, keepdims=True)
        h = xx * lax.rsqrt(ms + EPS) * gain1 + sh1
        hbuf[jp * 16:(jp + 1) * 16, :] = h.astype(BF16)

    hb = hbuf[...]
    zpbuf[hp8:hp8 + t, :] = _dot(hb, win_ref[:, 0:w_pool])
    n_cc = 4
    cc = d // n_cc
    for c in range(n_cc):
        val = _dot(hb, win_ref[:, w_pool + c * cc:w_pool + (c + 1) * cc])
        gt = _dot(hb, win_ref[:, w_pool + d + c * cc:w_pool + d + (c + 1) * cc])
        ubuf[hu8:hu8 + t, c * cc:(c + 1) * cc] = val * _sigmoid(gt)
    zga[...] = _dot(hb, win_ref[:, w_pool + 2 * d:w_pool + 3 * d])
    zgb[...] = _dot(hb, win_ref[:, w_pool + 3 * d:w_pool + 4 * d])

    z_tail = zpbuf[hp8 + t - hp8:hp8 + t, :]
    zpbuf[0:hp8, :] = _halo_groups(z_tail, zcarry[...])
    zcarry[...] = z_tail

    def pool_rows(r0):
        for g, w in enumerate(POOL_WINDOWS):
            cols = slice(g * pg, (g + 1) * pg)
            zc = zpbuf[pl.ds(r0 + hp8, ROW_CHUNK), cols]
            s = zc
            for k in range(1, w):
                s = s + zpbuf[pl.ds(r0 + hp8 - k * SUBLANES, ROW_CHUNK), cols]
            pooled = s * invc[pl.ds(r0, ROW_CHUNK), cols] - zc
            pbuf[pl.ds(r0, ROW_CHUNK), cols] = pooled.astype(BF16)
    _row_loop(t, ROW_CHUNK, pool_rows)

    for g in range(n_grp):
        cols = slice(g * pg, (g + 1) * pg)
        pa = _dot(pbuf[:, cols], wgrp_ref[g]) + bgrp_ref[:, cols]
        pbuf[:, cols] = (pa * pscale_ref[:, cols]).astype(BF16)
    ya[...] = _dot(pbuf[...], wpo_ref[...])

    u_tail = ubuf[hu8 + t - hu8:hu8 + t, :]
    ubuf[0:hu8, :] = _halo_groups(u_tail, ucarry[...])
    ucarry[...] = u_tail

    for lb in range(d // LANES):
        cols = slice(lb * LANES, (lb + 1) * LANES)

        def conv_rows(r0, cols=cols):
            acc = jnp.zeros((ROW_CHUNK, LANES), F32) + cb_ref[:, cols]
            for k in range(conv_k):
                acc = acc + cw_ref[k:k + 1, cols] * ubuf[pl.ds(r0 + k * SUBLANES, ROW_CHUNK), cols]
            cbuf[pl.ds(r0, ROW_CHUNK), cols] = acc
        _row_loop(t, ROW_CHUNK, conv_rows)

    def ln_rows(r0):
        v = cbuf[pl.ds(r0, 32), :]
        mu = jnp.mean(v, axis=-1, keepdims=True)
        vc = v - mu
        var = jnp.mean(vc * vc, axis=-1, keepdims=True)
        y = vc * lax.rsqrt(var + EPS) * lng_ref[...] + lnb_ref[...]
        act[pl.ds(r0, 32), :] = (y * _sigmoid(y)).astype(BF16)
    _row_loop(t, 32, ln_rows)

    cbuf[...] = _dot(act[...], wco_ref[...])

    def merge_rows(r0):
        rows = pl.ds(r0, 32)
        m = _sigmoid(zga[rows, :]) * ya[rows, :] + _sigmoid(zgb[rows, :]) * cbuf[rows, :]
        hbuf[rows, :] = m.astype(BF16)
    _row_loop(t, 32, merge_rows)

    ya[...] = _dot(hbuf[...], wout_ref[...])

    for j in range(l):
        rows = slice(j * SUBLANES, (j + 1) * SUBLANES)
        cols = slice(j * d, (j + 1) * d)
        o_ref[:, cols] = x_ref[:, cols] + ga1 * ya[rows, :]


def _const_spec(shape):
    nd = len(shape)
    return pl.BlockSpec(shape, lambda b, i: (0,) * nd, pipeline_mode=pl.Buffered(1))


def _mix_call(xr, mod, g1, w_in, w_grp, b_grp, pscale, w_po, cw, cb, lng, lnb, w_co, w_out):
    bsz, n_tiles, _, ld = xr.shape
    d = g1.shape[-1]
    t = TOKENS_PER_TILE
    w_pool = w_po.shape[0]
    conv_k = cw.shape[0]
    hp8 = max(POOL_WINDOWS) * SUBLANES
    hu8 = (conv_k - 1) * SUBLANES
    kern = functools.partial(_mix_kernel, d=d, w_pool=w_pool, conv_k=conv_k, t=t)
    x_spec = pl.BlockSpec((None, None, SUBLANES, ld), lambda b, i: (b, i, 0, 0))
    weights = (g1, w_in, w_grp, b_grp, pscale, w_po, cw, cb, lng, lnb, w_co, w_out)
    return pl.pallas_call(
        kern,
        grid=(bsz, n_tiles),
        in_specs=[x_spec, pl.BlockSpec((None, N_MOD, d), lambda b, i: (b, 0, 0))]
                 + [_const_spec(w.shape) for w in weights],
        out_specs=x_spec,
        out_shape=jax.ShapeDtypeStruct(xr.shape, F32),
        scratch_shapes=[
            pltpu.VMEM((t, d), BF16),
            pltpu.VMEM((hp8 + t, w_pool), F32),
            pltpu.VMEM((hu8 + t, d), F32),
            pltpu.VMEM((t, d), F32),
            pltpu.VMEM((t, d), F32),
            pltpu.VMEM((t, w_pool), F32),
            pltpu.VMEM((t, w_pool), BF16),
            pltpu.VMEM((t, d), F32),
            pltpu.VMEM((t, d), F32),
            pltpu.VMEM((t, d), BF16),
            pltpu.VMEM((hp8, w_pool), F32),
            pltpu.VMEM((hu8, d), F32),
        ],
        compiler_params=pltpu.CompilerParams(
            dimension_semantics=("arbitrary", "arbitrary"),
            vmem_limit_bytes=VMEM_LIMIT_BYTES),
        name="token_mix",
    )(xr, mod, *weights)


def _ffn_kernel(x_ref, mod_ref, g2_ref, wup_ref, fcw_ref, fcb_ref, wdn_ref, gf_ref,
                o_ref,
                hbuf, upbuf, fbuf, acc, fcarry,
                *, d, ffn_k, t, n_chunks):
    i = pl.program_id(1)
    l = t // SUBLANES
    fc = FFN_CHUNK
    hf8 = (ffn_k - 1) * SUBLANES

    @pl.when(i == 0)
    def _():
        fcarry[...] = jnp.zeros_like(fcarry)

    sh2 = mod_ref[3:4, :]
    gain2 = g2_ref[...] * (1.0 + mod_ref[4:5, :])
    ga2 = mod_ref[5:6, :]

    for jp in range(l // 2):
        xa = x_ref[:, (2 * jp) * d:(2 * jp + 1) * d]
        xb = x_ref[:, (2 * jp + 1) * d:(2 * jp + 2) * d]
        xx = jnp.concatenate([xa, xb], axis=0)
        ms = jnp.mean(xx * xx, axis=-1, keepdims=True)
        h = xx * lax.rsqrt(ms + EPS) * gain2 + sh2
        hbuf[jp * 16:(jp + 1) * 16, :] = h.astype(BF16)

    acc[...] = jnp.zeros_like(acc)

    def chunk_step(c, carry):
        upbuf[hf8:hf8 + t, :] = _dot(hbuf[...], wup_ref[c])
        tail = upbuf[t:hf8 + t, :]
        upbuf[0:hf8, :] = _halo_groups(tail, fcarry[c])
        fcarry[c] = tail
        w = fcw_ref[c]
        bias = fcb_ref[c]

        def conv_rows(r0):
            y = jnp.zeros((ROW_CHUNK, 2 * fc), F32) + bias
            for k in range(ffn_k):
                y = y + w[k:k + 1, :] * upbuf[pl.ds(r0 + k * SUBLANES, ROW_CHUNK), :]
            yg = y[:, :fc]
            fbuf[pl.ds(r0, ROW_CHUNK), :] = (yg * _sigmoid(yg) * y[:, fc:]).astype(BF16)
        _row_loop(t, ROW_CHUNK, conv_rows)
        acc[...] += _dot(fbuf[...], wdn_ref[c])
        return carry
    lax.fori_loop(0, n_chunks, chunk_step, 0)

    for j in range(l):
        rows = slice(j * SUBLANES, (j + 1) * SUBLANES)
        cols = slice(j * d, (j + 1) * d)
        x2 = x_ref[:, cols] + ga2 * acc[rows, :]
        ms = jnp.mean(x2 * x2, axis=-1, keepdims=True)
        o_ref[:, cols] = x2 * lax.rsqrt(ms + EPS) * gf_ref[...]


def _ffn_call(xr, mod, g2, w_up_c, fcw_c, fcb_c, w_dn_c, g_final):
    bsz, n_tiles, _, ld = xr.shape
    d = g2.shape[-1]
    t = TOKENS_PER_TILE
    n_chunks, ffn_k, _ = fcw_c.shape
    hf8 = (ffn_k - 1) * SUBLANES
    kern = functools.partial(_ffn_kernel, d=d, ffn_k=ffn_k, t=t, n_chunks=n_chunks)
    x_spec = pl.BlockSpec((None, None, SUBLANES, ld), lambda b, i: (b, i, 0, 0))
    weights = (g2, w_up_c, fcw_c, fcb_c, w_dn_c, g_final)
    return pl.pallas_call(
        kern,
        grid=(bsz, n_tiles),
        in_specs=[x_spec, pl.BlockSpec((None, N_MOD, d), lambda b, i: (b, 0, 0))]
                 + [_const_spec(w.shape) for w in weights],
        out_specs=x_spec,
        out_shape=jax.ShapeDtypeStruct(xr.shape, F32),
        scratch_shapes=[
            pltpu.VMEM((t, d), BF16),
            pltpu.VMEM((hf8 + t, 2 * FFN_CHUNK), F32),
            pltpu.VMEM((t, FFN_CHUNK), BF16),
            pltpu.VMEM((t, d), F32),
            pltpu.VMEM((n_chunks, hf8, 2 * FFN_CHUNK), F32),
        ],
        compiler_params=pltpu.CompilerParams(
            dimension_semantics=("arbitrary", "arbitrary"),
            vmem_limit_bytes=VMEM_LIMIT_BYTES),
        name="conv_ffn",
    )(xr, mod, *weights)


def _chunk_cols(a, n_chunks):
    f = a.shape[-1] // 2
    lead = a.shape[:-1]
    g = a[..., :f].reshape(lead + (n_chunks, FFN_CHUNK))
    v = a[..., f:].reshape(lead + (n_chunks, FFN_CHUNK))
    gv = jnp.concatenate([g, v], axis=-1)
    return jnp.moveaxis(gv, -2, 0)


def kernel(x, c, w_ada, b_ada, g_norm1, w_in, w_pool_grp, b_pool_grp, pool_scale, w_pool_out, conv_w, conv_b, ln_g, ln_b, w_conv_out, w_out, g_norm2, w_up, ffn_conv_w, ffn_conv_b, w_down, g_final):
    bsz, seq, d = x.shape
    depth = w_in.shape[0]
    t = TOKENS_PER_TILE
    d_ff = w_down.shape[1]
    n_chunks = d_ff // FFN_CHUNK
    assert seq % t == 0 and d_ff % FFN_CHUNK == 0 and t // SUBLANES >= conv_w.shape[1] - 1

    xr = x.reshape(bsz, seq // t, SUBLANES, (t // SUBLANES) * d)
    for l in range(depth):
        mod = _ada_call(c, w_ada[l], b_ada[l]).reshape(bsz, N_MOD, d)
        xr = _mix_call(
            xr, mod, g_norm1[l][None], w_in[l].astype(BF16), w_pool_grp[l].astype(BF16),
            b_pool_grp[l].reshape(1, -1), pool_scale[l][None], w_pool_out[l].astype(BF16),
            conv_w[l], conv_b[l][None], ln_g[l][None], ln_b[l][None],
            w_conv_out[l].astype(BF16), w_out[l].astype(BF16))
        is_last = l == depth - 1
        assert is_last or depth == 1
        xr = _ffn_call(
            xr, mod, g_norm2[l][None], _chunk_cols(w_up[l], n_chunks).astype(BF16),
            _chunk_cols(ffn_conv_w[l], n_chunks), _chunk_cols(ffn_conv_b[l][None], n_chunks),
            w_down[l].reshape(n_chunks, FFN_CHUNK, d).astype(BF16), g_final[None])
    return xr.reshape(bsz, seq, d)
```

```python
import functools

import jax
import jax.numpy as jnp
from jax import lax
from jax.experimental import pallas as pl
from jax.experimental.pallas import tpu as pltpu

F32 = jnp.float32
BF16 = jnp.bfloat16

EPS = 1e-6
POOL_WINDOWS = (2, 4, 8, 16)
N_MOD = 6

SUBLANES = 8
LANES = 128
MXU_COLS = 256
TOKENS_PER_TILE = 512
ROW_CHUNK = 64
NORM_ROWS = 16
ADA_COLS = 512
VMEM_LIMIT_BYTES = 60 * 1024 * 1024


def _dot(a, b):
    return jnp.dot(a, b, preferred_element_type=F32)


def _sigmoid(x):
    return jax.nn.sigmoid(x)


def _halo_groups(cur_tail, prev_tail):
    n = cur_tail.shape[0] // SUBLANES
    sub = lax.broadcasted_iota(jnp.int32, cur_tail.shape, 0) % SUBLANES
    mixed = jnp.where(sub == SUBLANES - 1, prev_tail, cur_tail)
    groups = [pltpu.roll(mixed[g * SUBLANES:(g + 1) * SUBLANES], 1, 0) for g in range(n)]
    return jnp.concatenate(groups, axis=0) if n > 1 else groups[0]


def _norm_modulate_to_bf16(x_ref, hbuf, gain, shift, d, n_groups):
    for jp in range(n_groups // 2):
        xa = x_ref[:, (2 * jp) * d:(2 * jp + 1) * d]
        xb = x_ref[:, (2 * jp + 1) * d:(2 * jp + 2) * d]
        xx = jnp.concatenate([xa, xb], axis=0)
        ms = jnp.mean(xx * xx, axis=-1, keepdims=True)
        h = xx * lax.rsqrt(ms + EPS) * gain + shift
        hbuf[jp * 16:(jp + 1) * 16, :] = h.astype(BF16)


def _ada_kernel(c_ref, w_ref, b_ref, o_ref):
    c = c_ref[...]
    ca = c * _sigmoid(c)
    w = w_ref[...]
    ca_hi = ca.astype(BF16)
    ca_lo = (ca - ca_hi.astype(F32)).astype(BF16)
    w_hi = w.astype(BF16)
    w_lo = (w - w_hi.astype(F32)).astype(BF16)
    acc = _dot(ca_hi, w_hi) + _dot(ca_lo, w_hi) + _dot(ca_hi, w_lo)
    o_ref[...] = acc + b_ref[...]


def _ada_call(c, w_ada, b_ada):
    b, d = c.shape
    n = w_ada.shape[1]
    return pl.pallas_call(
        _ada_kernel,
        grid=(n // ADA_COLS,),
        in_specs=[
            pl.BlockSpec((b, d), lambda j: (0, 0)),
            pl.BlockSpec((d, ADA_COLS), lambda j: (0, j)),
            pl.BlockSpec((1, ADA_COLS), lambda j: (0, j)),
        ],
        out_specs=pl.BlockSpec((b, ADA_COLS), lambda j: (0, j)),
        out_shape=jax.ShapeDtypeStruct((b, n), F32),
        compiler_params=pltpu.CompilerParams(dimension_semantics=("arbitrary",)),
        name="adaln_mod",
    )(c, w_ada, b_ada.reshape(1, n))


def _mix_kernel(x_ref, mod_ref, g1_ref, win_ref, wgrp_ref, bgrp_ref, pscale_ref, wpo_ref,
                cw_ref, cb_ref, lng_ref, lnb_ref, wco_ref, wout_ref,
                o_ref,
                hbuf, zpbuf, uslab, zga, zgb, invc, pbuf, ya, cbuf, act, zcarry, ucarry,
                *, d, w_pool, conv_k, t):
    i = pl.program_id(1)
    l = t // SUBLANES
    n_grp = len(POOL_WINDOWS)
    pg = w_pool // n_grp
    hp8 = max(POOL_WINDOWS) * SUBLANES
    hu8 = (conv_k - 1) * SUBLANES
    n_rc = t // ROW_CHUNK

    @pl.when(i == 0)
    def _():
        zcarry[...] = jnp.zeros_like(zcarry)
        ucarry[...] = jnp.zeros_like(ucarry)
        r = lax.broadcasted_iota(jnp.int32, (t, pg), 0)
        tok = (r % SUBLANES) * l + r // SUBLANES
        for g, w in enumerate(POOL_WINDOWS):
            cnt = jnp.minimum(tok + 1, w).astype(F32)
            invc[:, g * pg:(g + 1) * pg] = 1.0 / cnt

    @pl.when(i == 1)
    def _():
        for g, w in enumerate(POOL_WINDOWS):
            invc[:, g * pg:(g + 1) * pg] = jnp.full((t, pg), 1.0 / w, F32)

    sh1 = mod_ref[0:1, :]
    gain1 = g1_ref[...] * (1.0 + mod_ref[1:2, :])
    ga1 = mod_ref[2:3, :]

    _norm_modulate_to_bf16(x_ref, hbuf, gain1, sh1, d, l)

    def glu_chunk(c):
        lo = c * MXU_COLS
        val = _dot(hbuf[...], win_ref[:, w_pool + lo:w_pool + lo + MXU_COLS])
        gt = _dot(hbuf[...], win_ref[:, w_pool + d + lo:w_pool + d + lo + MXU_COLS])
        for rc in range(n_rc):
            rows = slice(rc * ROW_CHUNK, (rc + 1) * ROW_CHUNK)
            u = val[rows] * _sigmoid(gt[rows])
            for h in range(MXU_COLS // LANES):
                lb = c * (MXU_COLS // LANES) + h
                uslab[lb, hu8 + rc * ROW_CHUNK:hu8 + (rc + 1) * ROW_CHUNK, :] = (
                    u[:, h * LANES:(h + 1) * LANES])

    def conv_block(lb):
        cols = slice(lb * LANES, (lb + 1) * LANES)
        tail = uslab[lb, t:hu8 + t, :]
        uslab[lb, 0:hu8, :] = _halo_groups(tail, ucarry[lb])
        ucarry[lb] = tail
        for rc in range(n_rc):
            r0 = rc * ROW_CHUNK
            acc = cw_ref[0:1, cols] * uslab[lb, r0:r0 + ROW_CHUNK, :] + cb_ref[:, cols]
            for k in range(1, conv_k):
                acc = acc + cw_ref[k:k + 1, cols] * uslab[lb, r0 + k * SUBLANES:r0 + k * SUBLANES + ROW_CHUNK, :]
            cbuf[r0:r0 + ROW_CHUNK, cols] = acc

    def pool_stage():
        z_tail = zpbuf[t:hp8 + t, :]
        zpbuf[0:hp8, :] = _halo_groups(z_tail, zcarry[...])
        zcarry[...] = z_tail
        for rc in range(n_rc):
            r0 = rc * ROW_CHUNK
            for g, w in enumerate(POOL_WINDOWS):
                cols = slice(g * pg, (g + 1) * pg)
                zc = zpbuf[hp8 + r0:hp8 + r0 + ROW_CHUNK, cols]
                s = zc
                for k in range(1, w):
                    s = s + zpbuf[hp8 + r0 - k * SUBLANES:hp8 + r0 - k * SUBLANES + ROW_CHUNK, cols]
                pooled = s * invc[r0:r0 + ROW_CHUNK, cols] - zc
                pbuf[r0:r0 + ROW_CHUNK, cols] = pooled.astype(BF16)

    n_cc = d // MXU_COLS
    lb_per_cc = MXU_COLS // LANES
    glu_chunk(0)
    for c in range(1, n_cc):
        glu_chunk(c)
        for h in range(lb_per_cc):
            conv_block((c - 1) * lb_per_cc + h)
    zpbuf[hp8:hp8 + t, :] = _dot(hbuf[...], win_ref[:, 0:w_pool])
    zga[...] = _dot(hbuf[...], win_ref[:, w_pool + 2 * d:w_pool + 3 * d])
    for h in range(lb_per_cc):
        conv_block((n_cc - 1) * lb_per_cc + h)
    zgb[...] = _dot(hbuf[...], win_ref[:, w_pool + 3 * d:w_pool + 4 * d])
    pool_stage()

    for g in range(n_grp):
        cols = slice(g * pg, (g + 1) * pg)
        pa = _dot(pbuf[:, cols], wgrp_ref[g]) + bgrp_ref[:, cols]
        pbuf[:, cols] = (pa * pscale_ref[:, cols]).astype(BF16)
    ya[...] = _dot(pbuf[...], wpo_ref[...])

    for rc in range(t // NORM_ROWS):
        rows = slice(rc * NORM_ROWS, (rc + 1) * NORM_ROWS)
        v = cbuf[rows, :]
        mu = jnp.mean(v, axis=-1, keepdims=True)
        vc = v - mu
        var = jnp.mean(vc * vc, axis=-1, keepdims=True)
        y = vc * lax.rsqrt(var + EPS) * lng_ref[...] + lnb_ref[...]
        act[rows, :] = (y * _sigmoid(y)).astype(BF16)
    cbuf[...] = _dot(act[...], wco_ref[...])

    for rc in range(t // NORM_ROWS):
        rows = slice(rc * NORM_ROWS, (rc + 1) * NORM_ROWS)
        m = _sigmoid(zga[rows, :]) * ya[rows, :] + _sigmoid(zgb[rows, :]) * cbuf[rows, :]
        hbuf[rows, :] = m.astype(BF16)
    ya[...] = _dot(hbuf[...], wout_ref[...])

    for j in range(l):
        rows = slice(j * SUBLANES, (j + 1) * SUBLANES)
        cols = slice(j * d, (j + 1) * d)
        o_ref[:, cols] = x_ref[:, cols] + ga1 * ya[rows, :]


def _const_spec(shape):
    nd = len(shape)
    return pl.BlockSpec(shape, lambda b, i: (0,) * nd, pipeline_mode=pl.Buffered(1))


def _mix_call(xr, mod, g1, w_in, w_grp, b_grp, pscale, w_po, cw, cb, lng, lnb, w_co, w_out):
    bsz, n_tiles, _, ld = xr.shape
    d = g1.shape[-1]
    t = TOKENS_PER_TILE
    w_pool = w_po.shape[0]
    conv_k = cw.shape[0]
    hp8 = max(POOL_WINDOWS) * SUBLANES
    hu8 = (conv_k - 1) * SUBLANES
    kern = functools.partial(_mix_kernel, d=d, w_pool=w_pool, conv_k=conv_k, t=t)
    x_spec = pl.BlockSpec((None, None, SUBLANES, ld), lambda b, i: (b, i, 0, 0))
    weights = (g1, w_in, w_grp, b_grp, pscale, w_po, cw, cb, lng, lnb, w_co, w_out)
    return pl.pallas_call(
        kern,
        grid=(bsz, n_tiles),
        in_specs=[x_spec, pl.BlockSpec((None, N_MOD, d), lambda b, i: (b, 0, 0))]
                 + [_const_spec(w.shape) for w in weights],
        out_specs=x_spec,
        out_shape=jax.ShapeDtypeStruct(xr.shape, F32),
        scratch_shapes=[
            pltpu.VMEM((t, d), BF16),
            pltpu.VMEM((hp8 + t, w_pool), F32),
            pltpu.VMEM((d // LANES, hu8 + t, LANES), F32),
            pltpu.VMEM((t, d), F32),
            pltpu.VMEM((t, d), F32),
            pltpu.VMEM((t, w_pool), F32),
            pltpu.VMEM((t, w_pool), BF16),
            pltpu.VMEM((t, d), F32),
            pltpu.VMEM((t, d), F32),
            pltpu.VMEM((t, d), BF16),
            pltpu.VMEM((hp8, w_pool), F32),
            pltpu.VMEM((d // LANES, hu8, LANES), F32),
        ],
        compiler_params=pltpu.CompilerParams(
            dimension_semantics=("arbitrary", "arbitrary"),
            vmem_limit_bytes=VMEM_LIMIT_BYTES),
        name="token_mix",
    )(xr, mod, *weights)


def _ffn_kernel(x_ref, mod_ref, g2_ref, wup_ref, fcw_ref, fcb_ref, wdn_ref, gf_ref,
                o_ref,
                hbuf, upg, upv, fbuf, acc, gcarry, vcarry,
                *, d, d_ff, ffn_k, t):
    i = pl.program_id(1)
    l = t // SUBLANES
    fc = MXU_COLS
    hf8 = (ffn_k - 1) * SUBLANES
    n_rc = t // ROW_CHUNK
    n_chunks = d_ff // fc

    @pl.when(i == 0)
    def _():
        gcarry[...] = jnp.zeros_like(gcarry)
        vcarry[...] = jnp.zeros_like(vcarry)

    sh2 = mod_ref[3:4, :]
    gain2 = g2_ref[...] * (1.0 + mod_ref[4:5, :])
    ga2 = mod_ref[5:6, :]

    _norm_modulate_to_bf16(x_ref, hbuf, gain2, sh2, d, l)

    def conv_rows(buf, p, r0, cols):
        y = fcw_ref[0:1, cols] * buf[p, r0:r0 + ROW_CHUNK, :] + fcb_ref[:, cols]
        for k in range(1, ffn_k):
            y = y + fcw_ref[k:k + 1, cols] * buf[p, r0 + k * SUBLANES:r0 + k * SUBLANES + ROW_CHUNK, :]
        return y

    for c in range(n_chunks):
        p = c % 2
        gcols = slice(c * fc, (c + 1) * fc)
        vcols = slice(d_ff + c * fc, d_ff + (c + 1) * fc)
        upg[p, hf8:hf8 + t, :] = _dot(hbuf[...], wup_ref[:, gcols])
        upv[p, hf8:hf8 + t, :] = _dot(hbuf[...], wup_ref[:, vcols])
        for buf, carry in ((upg, gcarry), (upv, vcarry)):
            tail = buf[p, t:hf8 + t, :]
            buf[p, 0:hf8, :] = _halo_groups(tail, carry[c])
            carry[c] = tail
        for rc in range(n_rc):
            r0 = rc * ROW_CHUNK
            yg = conv_rows(upg, p, r0, gcols)
            yv = conv_rows(upv, p, r0, vcols)
            fbuf[p, r0:r0 + ROW_CHUNK, :] = (yg * _sigmoid(yg) * yv).astype(BF16)
        down = _dot(fbuf[p], wdn_ref[c * fc:(c + 1) * fc, :])
        if c == 0:
            acc[...] = down
        else:
            acc[...] += down

    for j in range(l):
        rows = slice(j * SUBLANES, (j + 1) * SUBLANES)
        cols = slice(j * d, (j + 1) * d)
        x2 = x_ref[:, cols] + ga2 * acc[rows, :]
        ms = jnp.mean(x2 * x2, axis=-1, keepdims=True)
        o_ref[:, cols] = x2 * lax.rsqrt(ms + EPS) * gf_ref[...]


def _ffn_call(xr, mod, g2, w_up, fcw, fcb, w_dn, g_final):
    bsz, n_tiles, _, ld = xr.shape
    d = g2.shape[-1]
    t = TOKENS_PER_TILE
    d_ff = w_dn.shape[0]
    ffn_k = fcw.shape[0]
    hf8 = (ffn_k - 1) * SUBLANES
    n_chunks = d_ff // MXU_COLS
    kern = functools.partial(_ffn_kernel, d=d, d_ff=d_ff, ffn_k=ffn_k, t=t)
    x_spec = pl.BlockSpec((None, None, SUBLANES, ld), lambda b, i: (b, i, 0, 0))
    weights = (g2, w_up, fcw, fcb, w_dn, g_final)
    return pl.pallas_call(
        kern,
        grid=(bsz, n_tiles),
        in_specs=[x_spec, pl.BlockSpec((None, N_MOD, d), lambda b, i: (b, 0, 0))]
                 + [_const_spec(w.shape) for w in weights],
        out_specs=x_spec,
        out_shape=jax.ShapeDtypeStruct(xr.shape, F32),
        scratch_shapes=[
            pltpu.VMEM((t, d), BF16),
            pltpu.VMEM((2, hf8 + t, MXU_COLS), F32),
            pltpu.VMEM((2, hf8 + t, MXU_COLS), F32),
            pltpu.VMEM((2, t, MXU_COLS), BF16),
            pltpu.VMEM((t, d), F32),
            pltpu.VMEM((n_chunks, hf8, MXU_COLS), F32),
            pltpu.VMEM((n_chunks, hf8, MXU_COLS), F32),
        ],
        compiler_params=pltpu.CompilerParams(
            dimension_semantics=("arbitrary", "arbitrary"),
            vmem_limit_bytes=VMEM_LIMIT_BYTES),
        name="conv_ffn",
    )(xr, mod, *weights)


def kernel(x, c, w_ada, b_ada, g_norm1, w_in, w_pool_grp, b_pool_grp, pool_scale, w_pool_out, conv_w, conv_b, ln_g, ln_b, w_conv_out, w_out, g_norm2, w_up, ffn_conv_w, ffn_conv_b, w_down, g_final):
    bsz, seq, d = x.shape
    depth = w_in.shape[0]
    t = TOKENS_PER_TILE
    d_ff = w_down.shape[1]
    assert depth == 1, "the final RMSNorm is fused into the (single) ConvFFN call"
    assert seq % t == 0 and d_ff % MXU_COLS == 0 and d % MXU_COLS == 0
    assert t // SUBLANES >= conv_w.shape[1] - 1

    xr = x.reshape(bsz, seq // t, SUBLANES, (t // SUBLANES) * d)
    for l in range(depth):
        mod = _ada_call(c, w_ada[l], b_ada[l]).reshape(bsz, N_MOD, d)
        xr = _mix_call(
            xr, mod, g_norm1[l][None], w_in[l].astype(BF16), w_pool_grp[l].astype(BF16),
            b_pool_grp[l].reshape(1, -1), pool_scale[l][None], w_pool_out[l].astype(BF16),
            conv_w[l], conv_b[l][None], ln_g[l][None], ln_b[l][None],
            w_conv_out[l].astype(BF16), w_out[l].astype(BF16))
        xr = _ffn_call(
            xr, mod, g_norm2[l][None], w_up[l].astype(BF16), ffn_conv_w[l],
            ffn_conv_b[l][None], w_down[l].astype(BF16), g_final[None])
    return xr.reshape(bsz, seq, d)
```

```python
import functools

import jax
import jax.numpy as jnp
from jax import lax
from jax.experimental import pallas as pl
from jax.experimental.pallas import tpu as pltpu

F32 = jnp.float32
BF16 = jnp.bfloat16

EPS = 1e-6
POOL_WINDOWS = (2, 4, 8, 16)
N_MOD = 6

SUBLANES = 8
LANES = 128
MXU_COLS = 256
TOKENS_PER_TILE = 512
ROW_CHUNK = 64
NORM_ROWS = 16
ROW_BLOCKS = 2
ADA_COLS = 512
VMEM_LIMIT_BYTES = 60 * 1024 * 1024


def _dot(a, b):
    return jnp.dot(a, b, preferred_element_type=F32)


def _sigmoid(x):
    return jax.nn.sigmoid(x)


def _interleave(*piece_lists):
    keyed = []
    for li, pieces in enumerate(piece_lists):
        for k, piece in enumerate(pieces):
            keyed.append((k / len(pieces), li, k, piece))
    for _, _, _, piece in sorted(keyed, key=lambda e: e[:3]):
        piece()


def _halo_groups(cur_tail, prev_tail):
    n = cur_tail.shape[0] // SUBLANES
    sub = lax.broadcasted_iota(jnp.int32, cur_tail.shape, 0) % SUBLANES
    mixed = jnp.where(sub == SUBLANES - 1, prev_tail, cur_tail)
    groups = [pltpu.roll(mixed[g * SUBLANES:(g + 1) * SUBLANES], 1, 0) for g in range(n)]
    return jnp.concatenate(groups, axis=0) if n > 1 else groups[0]


def _tile_copies(hbm, buf, sem, step, n_tiles, t, to_vmem):
    l = t // SUBLANES
    b = step // n_tiles
    row0 = (step % n_tiles) * t
    slot = step % 2
    copies = []
    for s in range(SUBLANES):
        h = hbm.at[b, pl.ds(row0 + s * l, l), :]
        v = buf.at[slot, :, s, :]
        src, dst = (h, v) if to_vmem else (v, h)
        copies.append(pltpu.make_async_copy(src, dst, sem.at[slot, s]))
    return copies


def _start(copies):
    for cp in copies:
        cp.start()


def _wait(copies):
    for cp in copies:
        cp.wait()


def _tile_prologue(x_hbm, o_hbm, xin, xout, sem_in, sem_out, step, n_steps, n_tiles, t):
    @pl.when(step == 0)
    def _():
        _start(_tile_copies(x_hbm, xin, sem_in, step, n_tiles, t, True))

    @pl.when(step + 1 < n_steps)
    def _():
        _start(_tile_copies(x_hbm, xin, sem_in, step + 1, n_tiles, t, True))

    _wait(_tile_copies(x_hbm, xin, sem_in, step, n_tiles, t, True))

    @pl.when(step >= 2)
    def _():
        _wait(_tile_copies(o_hbm, xout, sem_out, step - 2, n_tiles, t, False))


def _tile_epilogue(o_hbm, xout, sem_out, step, n_steps, n_tiles, t):
    _start(_tile_copies(o_hbm, xout, sem_out, step, n_tiles, t, False))

    @pl.when(step == n_steps - 1)
    def _():
        if n_steps >= 2:
            _wait(_tile_copies(o_hbm, xout, sem_out, step - 1, n_tiles, t, False))
        _wait(_tile_copies(o_hbm, xout, sem_out, step, n_tiles, t, False))


def _norm_modulate_to_bf16(xt, hbuf, gain, shift, n_groups):
    for jp in range(n_groups // 2):
        xx = jnp.concatenate([xt[2 * jp], xt[2 * jp + 1]], axis=0)
        ms = jnp.mean(xx * xx, axis=-1, keepdims=True)
        h = xx * lax.rsqrt(ms + EPS) * gain + shift
        hbuf[jp * 16:(jp + 1) * 16, :] = h.astype(BF16)


def _ada_kernel(c_ref, w_ref, b_ref, o_ref):
    c = c_ref[...]
    ca = c * _sigmoid(c)
    w = w_ref[...]
    ca_hi = ca.astype(BF16)
    ca_lo = (ca - ca_hi.astype(F32)).astype(BF16)
    w_hi = w.astype(BF16)
    w_lo = (w - w_hi.astype(F32)).astype(BF16)
    acc = _dot(ca_hi, w_hi) + _dot(ca_lo, w_hi) + _dot(ca_hi, w_lo)
    o_ref[...] = acc + b_ref[...]


def _ada_call(c, w_ada, b_ada):
    b, d = c.shape
    n = w_ada.shape[1]
    return pl.pallas_call(
        _ada_kernel,
        grid=(n // ADA_COLS,),
        in_specs=[
            pl.BlockSpec((b, d), lambda j: (0, 0)),
            pl.BlockSpec((d, ADA_COLS), lambda j: (0, j)),
            pl.BlockSpec((1, ADA_COLS), lambda j: (0, j)),
        ],
        out_specs=pl.BlockSpec((b, ADA_COLS), lambda j: (0, j)),
        out_shape=jax.ShapeDtypeStruct((b, n), F32),
        compiler_params=pltpu.CompilerParams(dimension_semantics=("arbitrary",)),
        name="adaln_mod",
    )(c, w_ada, b_ada.reshape(1, n))


def _mix_kernel(x_hbm, mod_ref, g1_ref, wzp_ref, wslab_ref, wgrp_ref, bgrp_ref, pscale_ref,
                wpo_ref, cw_ref, cb_ref, lng_ref, lnb_ref, wco_ref, wout_ref,
                o_hbm,
                xin, xout, sem_in, sem_out,
                hbuf, zpbuf, zraw0, zraw1, ubuf0, ubuf1, sga, sgb, cslab, invc, pbuf, ya, yb, obuf, act,
                zcarry, ucarry,
                *, d, w_pool, conv_k, t, n_tiles, n_steps):
    i = pl.program_id(1)
    step = pl.program_id(0) * n_tiles + i
    slot = step % 2
    l = t // SUBLANES
    n_grp = len(POOL_WINDOWS)
    pg = w_pool // n_grp
    hp8 = max(POOL_WINDOWS) * SUBLANES
    hu8 = (conv_k - 1) * SUBLANES
    n_rc = t // ROW_CHUNK
    n_lb = d // LANES

    _tile_prologue(x_hbm, o_hbm, xin, xout, sem_in, sem_out, step, n_steps, n_tiles, t)
    xt = xin.at[slot]
    ot = xout.at[slot]

    @pl.when(i == 0)
    def _():
        zcarry[...] = jnp.zeros_like(zcarry)
        ucarry[...] = jnp.zeros_like(ucarry)
        r = lax.broadcasted_iota(jnp.int32, (t, pg), 0)
        tok = (r % SUBLANES) * l + r // SUBLANES
        for g, w in enumerate(POOL_WINDOWS):
            cnt = jnp.minimum(tok + 1, w).astype(F32)
            invc[:, g * pg:(g + 1) * pg] = 1.0 / cnt

    @pl.when(i == 1)
    def _():
        for g, w in enumerate(POOL_WINDOWS):
            invc[:, g * pg:(g + 1) * pg] = jnp.full((t, pg), 1.0 / w, F32)

    sh1 = mod_ref[0:1, :]
    gain1 = g1_ref[...] * (1.0 + mod_ref[1:2, :])
    ga1 = mod_ref[2:3, :]

    def dot_pieces(n, zraw):
        def piece(nh, mh):
            rows = slice(mh * (t // 2), (mh + 1) * (t // 2))
            cols = slice(nh * MXU_COLS, (nh + 1) * MXU_COLS)
            zraw[rows, cols] = _dot(hbuf[rows, :], wslab_ref[n, :, cols])
        return [functools.partial(piece, nh, mh) for nh in range(2) for mh in range(2)]

    def glu_pieces(n, zraw, ubuf):
        def piece(rc):
            r0 = rc * ROW_CHUNK
            blk = zraw[r0:r0 + ROW_CHUNK, :]
            ubuf[hu8 + r0:hu8 + r0 + ROW_CHUNK, :] = (
                blk[:, 0:LANES] * _sigmoid(blk[:, LANES:2 * LANES]))
            sga[n, r0:r0 + ROW_CHUNK, :] = _sigmoid(blk[:, 2 * LANES:3 * LANES])
            sgb[n, r0:r0 + ROW_CHUNK, :] = _sigmoid(blk[:, 3 * LANES:4 * LANES])
        return [functools.partial(piece, rc) for rc in range(n_rc)]

    def conv_pieces(n, ubuf):
        def piece(rc):
            if rc == 0:
                tail = ubuf[t:hu8 + t, :]
                ubuf[0:hu8, :] = _halo_groups(tail, ucarry[n])
                ucarry[n] = tail
            r0 = rc * ROW_CHUNK
            n_out = ROW_CHUNK // SUBLANES
            win = [ubuf[r0 + g * SUBLANES:r0 + (g + 1) * SUBLANES, :]
                   for g in range(n_out + conv_k - 1)]
            bias = jnp.broadcast_to(cb_ref[n], (SUBLANES, LANES))
            accs = [bias] * n_out
            for k0 in range(0, conv_k, 4):
                ks = range(k0, min(k0 + 4, conv_k))
                wks = [jnp.broadcast_to(cw_ref[n, k:k + 1, :], (SUBLANES, LANES)) for k in ks]
                for j in range(n_out):
                    prods = [wk * win[j + k] for wk, k in zip(wks, ks)]
                    while len(prods) > 1:
                        prods = [a + b for a, b in zip(prods[::2], prods[1::2])] + (
                            [prods[-1]] if len(prods) % 2 else [])
                    accs[j] = accs[j] + prods[0]
            cslab[n, r0:r0 + ROW_CHUNK, :] = jnp.concatenate(accs, axis=0)
        return [functools.partial(piece, rc) for rc in range(n_rc)]

    def pool_stage():
        z_tail = zpbuf[t:hp8 + t, :]
        zpbuf[0:hp8, :] = _halo_groups(z_tail, zcarry[...])
        zcarry[...] = z_tail
        for rc in range(n_rc):
            r0 = rc * ROW_CHUNK
            for g, w in enumerate(POOL_WINDOWS):
                cols = slice(g * pg, (g + 1) * pg)
                zc = zpbuf[hp8 + r0:hp8 + r0 + ROW_CHUNK, cols]
                s = zc
                for k in range(1, w):
                    s = s + zpbuf[hp8 + r0 - k * SUBLANES:hp8 + r0 - k * SUBLANES + ROW_CHUNK, cols]
                pooled = s * invc[r0:r0 + ROW_CHUNK, cols] - zc
                pbuf[r0:r0 + ROW_CHUNK, cols] = pooled.astype(BF16)

    _norm_modulate_to_bf16(xt, hbuf, gain1, sh1, l)

    def zp_piece():
        zpbuf[hp8:hp8 + t, :] = _dot(hbuf[...], wzp_ref[...])

    def group_piece(g):
        cols = slice(g * pg, (g + 1) * pg)
        pa = _dot(pbuf[:, cols], wgrp_ref[g]) + bgrp_ref[:, cols]
        pbuf[:, cols] = (pa * pscale_ref[:, cols]).astype(BF16)

    def ya_piece(mh):
        rows = slice(mh * (t // 2), (mh + 1) * (t // 2))
        ya[rows, :] = _dot(pbuf[rows, :], wpo_ref[...])

    zp_piece()
    _interleave(dot_pieces(0, zraw0), [pool_stage])
    _interleave(dot_pieces(1, zraw1), glu_pieces(0, zraw0, ubuf0))

    def slab_pair_step(m, carry):
        n = 2 * m
        _interleave(dot_pieces(n, zraw0), conv_pieces(n - 2, ubuf0),
                    glu_pieces(n - 1, zraw1, ubuf1))
        _interleave(dot_pieces(n + 1, zraw1), conv_pieces(n - 1, ubuf1),
                    glu_pieces(n, zraw0, ubuf0))
        return carry
    lax.fori_loop(1, n_lb // 2, slab_pair_step, 0)

    _interleave([functools.partial(group_piece, g) for g in range(n_grp)],
                conv_pieces(n_lb - 2, ubuf0), glu_pieces(n_lb - 1, zraw1, ubuf1))
    _interleave([functools.partial(ya_piece, mh) for mh in range(2)],
                conv_pieces(n_lb - 1, ubuf1))

    rb = t // ROW_BLOCKS

    def ln_block(q):
        for rc in range(rb // NORM_ROWS):
            rows = slice(q * rb + rc * NORM_ROWS, q * rb + (rc + 1) * NORM_ROWS)
            v = jnp.concatenate([cslab[n, rows, :] for n in range(n_lb)], axis=-1)
            mu = jnp.mean(v, axis=-1, keepdims=True)
            vc = v - mu
            var = jnp.mean(vc * vc, axis=-1, keepdims=True)
            y = vc * lax.rsqrt(var + EPS) * lng_ref[...] + lnb_ref[...]
            act[rows, :] = (y * _sigmoid(y)).astype(BF16)

    def yb_block(q):
        rows = slice(q * rb, (q + 1) * rb)
        yb[rows, :] = _dot(act[rows, :], wco_ref[...])

    def merge_block(q):
        for rc in range(rb // NORM_ROWS):
            rows = slice(q * rb + rc * NORM_ROWS, q * rb + (rc + 1) * NORM_ROWS)
            ga = jnp.concatenate([sga[n, rows, :] for n in range(n_lb)], axis=-1)
            gb = jnp.concatenate([sgb[n, rows, :] for n in range(n_lb)], axis=-1)
            hbuf[rows, :] = (ga * ya[rows, :] + gb * yb[rows, :]).astype(BF16)

    def out_block(q):
        rows = slice(q * rb, (q + 1) * rb)
        obuf[rows, :] = _dot(hbuf[rows, :], wout_ref[...])

    def resid_block(q):
        for j in range(q * rb // SUBLANES, (q + 1) * rb // SUBLANES):
            ot[j] = xt[j] + ga1 * obuf[j * SUBLANES:(j + 1) * SUBLANES, :]

    stages = (ln_block, yb_block, merge_block, out_block, resid_block)
    for tick in range(ROW_BLOCKS + len(stages) - 1):
        for depth, stage in enumerate(stages):
            if 0 <= tick - depth < ROW_BLOCKS:
                stage(tick - depth)

    _tile_epilogue(o_hbm, xout, sem_out, step, n_steps, n_tiles, t)


def _const_spec(shape):
    nd = len(shape)
    return pl.BlockSpec(shape, lambda b, i: (0,) * nd, pipeline_mode=pl.Buffered(1))


def _mix_call(x, mod, g1, w_zp, w_slab, w_grp, b_grp, pscale, w_po, cw, cb, lng, lnb, w_co, w_out):
    bsz, seq, d = x.shape
    t = TOKENS_PER_TILE
    l = t // SUBLANES
    n_tiles = seq // t
    w_pool = w_po.shape[0]
    n_lb, conv_k, _ = cw.shape
    hp8 = max(POOL_WINDOWS) * SUBLANES
    hu8 = (conv_k - 1) * SUBLANES
    kern = functools.partial(_mix_kernel, d=d, w_pool=w_pool, conv_k=conv_k, t=t,
                             n_tiles=n_tiles, n_steps=bsz * n_tiles)
    weights = (g1, w_zp, w_slab, w_grp, b_grp, pscale, w_po, cw, cb, lng, lnb, w_co, w_out)
    return pl.pallas_call(
        kern,
        grid=(bsz, n_tiles),
        in_specs=[pl.BlockSpec(memory_space=pl.ANY),
                  pl.BlockSpec((None, N_MOD, d), lambda b, i: (b, 0, 0))]
                 + [_const_spec(w.shape) for w in weights],
        out_specs=pl.BlockSpec(memory_space=pl.ANY),
        out_shape=jax.ShapeDtypeStruct(x.shape, F32),
        scratch_shapes=[
            pltpu.VMEM((2, l, SUBLANES, d), F32),
            pltpu.VMEM((2, l, SUBLANES, d), F32),
            pltpu.SemaphoreType.DMA((2, SUBLANES)),
            pltpu.SemaphoreType.DMA((2, SUBLANES)),
            pltpu.VMEM((t, d), BF16),
            pltpu.VMEM((hp8 + t, w_pool), F32),
            pltpu.VMEM((t, 4 * LANES), F32),
            pltpu.VMEM((t, 4 * LANES), F32),
            pltpu.VMEM((hu8 + t, LANES), F32),
            pltpu.VMEM((hu8 + t, LANES), F32),
            pltpu.VMEM((n_lb, t, LANES), F32),
            pltpu.VMEM((n_lb, t, LANES), F32),
            pltpu.VMEM((n_lb, t, LANES), F32),
            pltpu.VMEM((t, w_pool), F32),
            pltpu.VMEM((t, w_pool), BF16),
            pltpu.VMEM((t, d), F32),
            pltpu.VMEM((t, d), F32),
            pltpu.VMEM((t, d), F32),
            pltpu.VMEM((t, d), BF16),
            pltpu.VMEM((hp8, w_pool), F32),
            pltpu.VMEM((n_lb, hu8, LANES), F32),
        ],
        compiler_params=pltpu.CompilerParams(
            dimension_semantics=("arbitrary", "arbitrary"),
            vmem_limit_bytes=VMEM_LIMIT_BYTES),
        name="token_mix",
    )(x, mod, *weights)


def _ffn_kernel(x_hbm, mod_ref, g2_ref, wup_ref, fcw_ref, fcb_ref, wdn_ref, gf_ref,
                o_hbm,
                xin, xout, sem_in, sem_out,
                hbuf, upg0, upv0, upg1, upv1, fbuf0, fbuf1, acc, gcarry, vcarry,
                *, d, d_ff, ffn_k, t, n_tiles, n_steps):
    i = pl.program_id(1)
    step = pl.program_id(0) * n_tiles + i
    slot = step % 2
    l = t // SUBLANES
    fc = MXU_COLS
    hf8 = (ffn_k - 1) * SUBLANES
    n_rc = t // ROW_CHUNK
    n_chunks = d_ff // fc

    _tile_prologue(x_hbm, o_hbm, xin, xout, sem_in, sem_out, step, n_steps, n_tiles, t)
    xt = xin.at[slot]
    ot = xout.at[slot]

    @pl.when(i == 0)
    def _():
        gcarry[...] = jnp.zeros_like(gcarry)
        vcarry[...] = jnp.zeros_like(vcarry)

    sh2 = mod_ref[3:4, :]
    gain2 = g2_ref[...] * (1.0 + mod_ref[4:5, :])
    ga2 = mod_ref[5:6, :]

    _norm_modulate_to_bf16(xt, hbuf, gain2, sh2, l)

    ups = ((upg0, upv0), (upg1, upv1))
    fbufs = (fbuf0, fbuf1)

    def up_pieces(c):
        def piece(which, mh):
            rows = slice(mh * (t // 2), (mh + 1) * (t // 2))
            lo = which * d_ff + c * fc
            ups[c % 2][which][hf8 + mh * (t // 2):hf8 + (mh + 1) * (t // 2), :] = _dot(
                hbuf[rows, :], wup_ref[:, lo:lo + fc])
        return [functools.partial(piece, which, mh) for which in range(2) for mh in range(2)]

    def conv_rows(buf, r0, cols):
        y = fcw_ref[0:1, cols] * buf[r0:r0 + ROW_CHUNK, :] + fcb_ref[:, cols]
        for k in range(1, ffn_k):
            y = y + fcw_ref[k:k + 1, cols] * buf[r0 + k * SUBLANES:r0 + k * SUBLANES + ROW_CHUNK, :]
        return y

    def gate_pieces(c):
        bg, bv = ups[c % 2]
        gcols = slice(c * fc, (c + 1) * fc)
        vcols = slice(d_ff + c * fc, d_ff + (c + 1) * fc)

        def piece(rc):
            if rc == 0:
                for buf, carry in ((bg, gcarry), (bv, vcarry)):
                    tail = buf[t:hf8 + t, :]
                    buf[0:hf8, :] = _halo_groups(tail, carry[c])
                    carry[c] = tail
            r0 = rc * ROW_CHUNK
            yg = conv_rows(bg, r0, gcols)
            yv = conv_rows(bv, r0, vcols)
            fbufs[c % 2][r0:r0 + ROW_CHUNK, :] = (yg * _sigmoid(yg) * yv).astype(BF16)
        return [functools.partial(piece, rc) for rc in range(n_rc)]

    def down_pieces(c):
        def piece(nh):
            cols = slice(nh * (d // 2), (nh + 1) * (d // 2))
            down = _dot(fbufs[c % 2][...], wdn_ref[c * fc:(c + 1) * fc, cols])
            if c == 0:
                acc[:, cols] = down
            else:
                acc[:, cols] += down
        return [functools.partial(piece, nh) for nh in range(2)]

    _interleave(up_pieces(0))
    for c in range(n_chunks):
        mxu = up_pieces(c + 1) if c + 1 < n_chunks else []
        mxu += down_pieces(c - 1) if c >= 1 else []
        _interleave(mxu, gate_pieces(c))
    _interleave(down_pieces(n_chunks - 1))

    for j in range(l):
        x2 = xt[j] + ga2 * acc[j * SUBLANES:(j + 1) * SUBLANES, :]
        ms = jnp.mean(x2 * x2, axis=-1, keepdims=True)
        ot[j] = x2 * lax.rsqrt(ms + EPS) * gf_ref[...]

    _tile_epilogue(o_hbm, xout, sem_out, step, n_steps, n_tiles, t)


def _ffn_call(x, mod, g2, w_up, fcw, fcb, w_dn, g_final):
    bsz, seq, d = x.shape
    t = TOKENS_PER_TILE
    l = t // SUBLANES
    n_tiles = seq // t
    d_ff = w_dn.shape[0]
    ffn_k = fcw.shape[0]
    hf8 = (ffn_k - 1) * SUBLANES
    n_chunks = d_ff // MXU_COLS
    kern = functools.partial(_ffn_kernel, d=d, d_ff=d_ff, ffn_k=ffn_k, t=t,
                             n_tiles=n_tiles, n_steps=bsz * n_tiles)
    weights = (g2, w_up, fcw, fcb, w_dn, g_final)
    return pl.pallas_call(
        kern,
        grid=(bsz, n_tiles),
        in_specs=[pl.BlockSpec(memory_space=pl.ANY),
                  pl.BlockSpec((None, N_MOD, d), lambda b, i: (b, 0, 0))]
                 + [_const_spec(w.shape) for w in weights],
        out_specs=pl.BlockSpec(memory_space=pl.ANY),
        out_shape=jax.ShapeDtypeStruct(x.shape, F32),
        scratch_shapes=[
            pltpu.VMEM((2, l, SUBLANES, d), F32),
            pltpu.VMEM((2, l, SUBLANES, d), F32),
            pltpu.SemaphoreType.DMA((2, SUBLANES)),
            pltpu.SemaphoreType.DMA((2, SUBLANES)),
            pltpu.VMEM((t, d), BF16),
            pltpu.VMEM((hf8 + t, MXU_COLS), F32),
            pltpu.VMEM((hf8 + t, MXU_COLS), F32),
            pltpu.VMEM((hf8 + t, MXU_COLS), F32),
            pltpu.VMEM((hf8 + t, MXU_COLS), F32),
            pltpu.VMEM((t, MXU_COLS), BF16),
            pltpu.VMEM((t, MXU_COLS), BF16),
            pltpu.VMEM((t, d), F32),
            pltpu.VMEM((n_chunks, hf8, MXU_COLS), F32),
            pltpu.VMEM((n_chunks, hf8, MXU_COLS), F32),
        ],
        compiler_params=pltpu.CompilerParams(
            dimension_semantics=("arbitrary", "arbitrary"),
            vmem_limit_bytes=VMEM_LIMIT_BYTES),
        name="conv_ffn",
    )(x, mod, *weights)


def _slab_cols(a, n_parts):
    r, cols = a.shape
    dd = cols // n_parts
    a4 = a.reshape(r, n_parts, dd // LANES, LANES)
    return jnp.transpose(a4, (2, 0, 1, 3)).reshape(dd // LANES, r, n_parts * LANES)


def kernel(x, c, w_ada, b_ada, g_norm1, w_in, w_pool_grp, b_pool_grp, pool_scale, w_pool_out, conv_w, conv_b, ln_g, ln_b, w_conv_out, w_out, g_norm2, w_up, ffn_conv_w, ffn_conv_b, w_down, g_final):
    bsz, seq, d = x.shape
    depth = w_in.shape[0]
    t = TOKENS_PER_TILE
    d_ff = w_down.shape[1]
    w_pool = w_pool_out.shape[1]
    assert depth == 1, "the final RMSNorm is fused into the (single) ConvFFN call"
    assert seq % t == 0 and d_ff % MXU_COLS == 0 and d % LANES == 0
    assert t // SUBLANES >= conv_w.shape[1] - 1 and (d // LANES) % 2 == 0 and d // LANES >= 4

    for l in range(depth):
        mod = _ada_call(c, w_ada[l], b_ada[l]).reshape(bsz, N_MOD, d)
        w_in_bf = w_in[l].astype(BF16)
        x = _mix_call(
            x, mod, g_norm1[l][None], w_in_bf[:, :w_pool], _slab_cols(w_in_bf[:, w_pool:], 4),
            w_pool_grp[l].astype(BF16), b_pool_grp[l].reshape(1, -1), pool_scale[l][None],
            w_pool_out[l].astype(BF16), _slab_cols(conv_w[l], 1), _slab_cols(conv_b[l][None], 1),
            ln_g[l][None], ln_b[l][None], w_conv_out[l].astype(BF16), w_out[l].astype(BF16))
        x = _ffn_call(
            x, mod, g_norm2[l][None], w_up[l].astype(BF16), ffn_conv_w[l],
            ffn_conv_b[l][None], w_down[l].astype(BF16), g_final[None])
    return x
```

```python
import functools

import jax
import jax.numpy as jnp
from jax import lax
from jax.experimental import pallas as pl
from jax.experimental.pallas import tpu as pltpu

F32 = jnp.float32
BF16 = jnp.bfloat16

EPS = 1e-6
POOL_WINDOWS = (2, 4, 8, 16)
N_MOD = 6

SUBLANES = 8
LANES = 128
MXU_COLS = 256
TOKENS_PER_TILE = 512
ROW_CHUNK = 64
NORM_ROWS = 16
ROW_BLOCKS = 2
ADA_COLS = 512
VMEM_LIMIT_BYTES = 60 * 1024 * 1024
CONV_ROWS = 64
N_IN_SLOTS = 3


def _dot(a, b):
    return jnp.dot(a, b, preferred_element_type=F32)


def _sigmoid(x):
    return jax.nn.sigmoid(x)


def _interleave(*piece_lists):
    keyed = []
    for li, pieces in enumerate(piece_lists):
        for k, piece in enumerate(pieces):
            keyed.append((k / len(pieces), li, k, piece))
    for _, _, _, piece in sorted(keyed, key=lambda e: e[:3]):
        piece()


def _halo_groups(cur_tail, prev_tail):
    n = cur_tail.shape[0] // SUBLANES
    sub = lax.broadcasted_iota(jnp.int32, cur_tail.shape, 0) % SUBLANES
    mixed = jnp.where(sub == SUBLANES - 1, prev_tail, cur_tail)
    groups = [pltpu.roll(mixed[g * SUBLANES:(g + 1) * SUBLANES], 1, 0) for g in range(n)]
    return jnp.concatenate(groups, axis=0) if n > 1 else groups[0]


def _tile_copies(hbm, buf, sem, step, n_tiles, t, to_vmem):
    l = t // SUBLANES
    b = step // n_tiles
    row0 = (step % n_tiles) * t
    slot = step % buf.shape[0]
    copies = []
    for s in range(SUBLANES):
        h = hbm.at[b, pl.ds(row0 + s * l, l), :]
        v = buf.at[slot, :, s, :]
        src, dst = (h, v) if to_vmem else (v, h)
        copies.append(pltpu.make_async_copy(src, dst, sem.at[slot, s]))
    return copies


def _start(copies):
    for cp in copies:
        cp.start()


def _wait(copies):
    for cp in copies:
        cp.wait()


def _first_tiles_load(x_hbm, xin, sem_in, n_steps, n_tiles, t):
    _start(_tile_copies(x_hbm, xin, sem_in, 0, n_tiles, t, True))
    if n_steps >= 2:
        _start(_tile_copies(x_hbm, xin, sem_in, 1, n_tiles, t, True))
    _wait(_tile_copies(x_hbm, xin, sem_in, 0, n_tiles, t, True))


def _tile_prologue(x_hbm, o_hbm, xin, xout, sem_in, sem_out, step, n_steps, n_tiles, t):
    @pl.when(step + 1 < n_steps)
    def _():
        _wait(_tile_copies(x_hbm, xin, sem_in, step + 1, n_tiles, t, True))

    @pl.when(step + 2 < n_steps)
    def _():
        _start(_tile_copies(x_hbm, xin, sem_in, step + 2, n_tiles, t, True))

    @pl.when(step >= 2)
    def _():
        _wait(_tile_copies(o_hbm, xout, sem_out, step - 2, n_tiles, t, False))


def _tile_epilogue(o_hbm, xout, sem_out, step, n_steps, n_tiles, t):
    _start(_tile_copies(o_hbm, xout, sem_out, step, n_tiles, t, False))

    @pl.when(step == n_steps - 1)
    def _():
        if n_steps >= 2:
            _wait(_tile_copies(o_hbm, xout, sem_out, step - 1, n_tiles, t, False))
        _wait(_tile_copies(o_hbm, xout, sem_out, step, n_tiles, t, False))


def _norm_pieces(xt, hbuf, gain, shift, n_groups, n_pieces=8):
    def piece(lo, hi):
        for jp in range(lo, hi):
            xx = jnp.concatenate([xt[2 * jp], xt[2 * jp + 1]], axis=0)
            ms = jnp.mean(xx * xx, axis=-1, keepdims=True)
            h = xx * lax.rsqrt(ms + EPS) * gain + shift
            hbuf[jp * 16:(jp + 1) * 16, :] = h.astype(BF16)
    per = n_groups // 2 // n_pieces
    return [functools.partial(piece, q * per, (q + 1) * per) for q in range(n_pieces)]


def _ada_kernel(c_ref, w_ref, b_ref, o_ref):
    c = c_ref[...]
    ca = c * _sigmoid(c)
    w = w_ref[...]
    ca_hi = ca.astype(BF16)
    ca_lo = (ca - ca_hi.astype(F32)).astype(BF16)
    w_hi = w.astype(BF16)
    w_lo = (w - w_hi.astype(F32)).astype(BF16)
    acc = _dot(ca_hi, w_hi) + _dot(ca_lo, w_hi) + _dot(ca_hi, w_lo)
    o_ref[...] = acc + b_ref[...]


def _ada_call(c, w_ada, b_ada):
    b, d = c.shape
    n = w_ada.shape[1]
    return pl.pallas_call(
        _ada_kernel,
        grid=(n // ADA_COLS,),
        in_specs=[
            pl.BlockSpec((b, d), lambda j: (0, 0)),
            pl.BlockSpec((d, ADA_COLS), lambda j: (0, j)),
            pl.BlockSpec((1, ADA_COLS), lambda j: (0, j)),
        ],
        out_specs=pl.BlockSpec((b, ADA_COLS), lambda j: (0, j)),
        out_shape=jax.ShapeDtypeStruct((b, n), F32),
        compiler_params=pltpu.CompilerParams(dimension_semantics=("arbitrary",)),
        name="adaln_mod",
    )(c, w_ada, b_ada.reshape(1, n))


def _mix_kernel(x_hbm, mod_ref, modall_ref, g1_ref, wzp_ref, wslab_ref, wgrp_ref, bgrp_ref,
                pscale_ref, wpo_ref, cw_ref, cb_ref, lng_ref, lnb_ref, wco_ref, wout_ref,
                o_hbm,
                xin, xout, sem_in, sem_out,
                hbuf, mbuf, zpbuf, zraw0, zraw1, ubuf0, ubuf1, sga, sgb, cslab, invc, pbuf, ya, yb,
                obuf, act, zcarry, ucarry,
                *, d, w_pool, conv_k, t, n_tiles, n_steps):
    i = pl.program_id(1)
    step = pl.program_id(0) * n_tiles + i
    nxt = jnp.minimum(step + 1, n_steps - 1)
    l = t // SUBLANES
    n_grp = len(POOL_WINDOWS)
    pg = w_pool // n_grp
    hp8 = max(POOL_WINDOWS) * SUBLANES
    hu8 = (conv_k - 1) * SUBLANES
    n_rc = t // ROW_CHUNK
    n_lb = d // LANES
    xt = xin.at[step % N_IN_SLOTS]
    ot = xout.at[step % 2]
    ga1 = mod_ref[2:3, :]

    def init_pooling_tables():
        zcarry[...] = jnp.zeros_like(zcarry)
        r = lax.broadcasted_iota(jnp.int32, (t, pg), 0)
        tok = (r % SUBLANES) * l + r // SUBLANES
        for g, w in enumerate(POOL_WINDOWS):
            cnt = jnp.minimum(tok + 1, w).astype(F32)
            invc[:, g * pg:(g + 1) * pg] = 1.0 / cnt

    def norm_pieces(x_tile, b_idx):
        shift = modall_ref[b_idx, 0:1, :]
        gain = g1_ref[...] * (1.0 + modall_ref[b_idx, 1:2, :])
        return _norm_pieces(x_tile, hbuf, gain, shift, l)

    def dot_pieces(n, zraw):
        def piece(nh, mh):
            rows = slice(mh * (t // 2), (mh + 1) * (t // 2))
            cols = slice(nh * MXU_COLS, (nh + 1) * MXU_COLS)
            zraw[rows, cols] = _dot(hbuf[rows, :], wslab_ref[n, :, cols])
        return [functools.partial(piece, nh, mh) for nh in range(2) for mh in range(2)]

    def glu_pieces(n, zraw, ubuf):
        def piece(rc):
            r0 = rc * ROW_CHUNK
            blk = zraw[r0:r0 + ROW_CHUNK, :]
            ubuf[hu8 + r0:hu8 + r0 + ROW_CHUNK, :] = (
                blk[:, 0:LANES] * _sigmoid(blk[:, LANES:2 * LANES]))
            sga[n, r0:r0 + ROW_CHUNK, :] = _sigmoid(blk[:, 2 * LANES:3 * LANES])
            sgb[n, r0:r0 + ROW_CHUNK, :] = _sigmoid(blk[:, 3 * LANES:4 * LANES])
        return [functools.partial(piece, rc) for rc in range(n_rc)]

    def conv_pieces(n, ubuf):
        def piece(rc):
            if rc == 0:
                tail = ubuf[t:hu8 + t, :]
                ubuf[0:hu8, :] = _halo_groups(tail, ucarry[n])
                ucarry[n] = tail
            r0 = rc * CONV_ROWS
            n_out = CONV_ROWS // SUBLANES
            win = [ubuf[r0 + g * SUBLANES:r0 + (g + 1) * SUBLANES, :]
                   for g in range(n_out + conv_k - 1)]
            bias = jnp.broadcast_to(cb_ref[n], (SUBLANES, LANES))
            accs = [bias] * n_out
            for k0 in range(0, conv_k, 4):
                ks = range(k0, min(k0 + 4, conv_k))
                wks = [jnp.broadcast_to(cw_ref[n, k:k + 1, :], (SUBLANES, LANES)) for k in ks]
                if k0:
                    bits = pltpu.bitcast(accs[n_out - 1], jnp.uint32)
                    zero = pltpu.bitcast((bits >> 16) >> 16, F32)
                    wks = [wk + zero for wk in wks]
                for j in range(n_out):
                    prods = [wk * win[j + k] for wk, k in zip(wks, ks)]
                    while len(prods) > 1:
                        prods = [a + b for a, b in zip(prods[::2], prods[1::2])] + (
                            [prods[-1]] if len(prods) % 2 else [])
                    accs[j] = accs[j] + prods[0]
            cslab[n, r0:r0 + CONV_ROWS, :] = jnp.concatenate(accs, axis=0)
        return [functools.partial(piece, rc) for rc in range(t // CONV_ROWS)]

    def pool_pieces(seq_start):
        def piece(rc):
            if rc == 0:
                z_tail = zpbuf[t:hp8 + t, :]
                prev = jnp.where(seq_start, 0.0, zcarry[...])
                zpbuf[0:hp8, :] = _halo_groups(z_tail, prev)
                zcarry[...] = z_tail
            r0 = rc * ROW_CHUNK
            for g, w in enumerate(POOL_WINDOWS):
                cols = slice(g * pg, (g + 1) * pg)
                zc = zpbuf[hp8 + r0:hp8 + r0 + ROW_CHUNK, cols]
                s = zc
                for k in range(1, w):
                    s = s + zpbuf[hp8 + r0 - k * SUBLANES:hp8 + r0 - k * SUBLANES + ROW_CHUNK, cols]
                inv = jnp.where(seq_start, invc[r0:r0 + ROW_CHUNK, cols], 1.0 / w)
                pbuf[r0:r0 + ROW_CHUNK, cols] = (s * inv - zc).astype(BF16)
        return [functools.partial(piece, rc) for rc in range(n_rc)]

    def zp_pieces():
        def piece(mh):
            rows = slice(mh * (t // 2), (mh + 1) * (t // 2))
            zpbuf[hp8 + mh * (t // 2):hp8 + (mh + 1) * (t // 2), :] = _dot(hbuf[rows, :], wzp_ref[...])
        return [functools.partial(piece, mh) for mh in range(2)]

    def group_piece(g):
        cols = slice(g * pg, (g + 1) * pg)
        pa = _dot(pbuf[:, cols], wgrp_ref[g]) + bgrp_ref[:, cols]
        pbuf[:, cols] = (pa * pscale_ref[:, cols]).astype(BF16)

    def ya_piece(mh):
        rows = slice(mh * (t // 2), (mh + 1) * (t // 2))
        ya[rows, :] = _dot(pbuf[rows, :], wpo_ref[...])

    def first_stage_mxu():
        return zp_pieces() + dot_pieces(0, zraw0) + dot_pieces(1, zraw1)

    @pl.when(step == 0)
    def _():
        _first_tiles_load(x_hbm, xin, sem_in, n_steps, n_tiles, t)
        init_pooling_tables()
        _interleave(norm_pieces(xin.at[0], 0))
        _interleave(first_stage_mxu())
        _interleave(pool_pieces(True))

    _tile_prologue(x_hbm, o_hbm, xin, xout, sem_in, sem_out, step, n_steps, n_tiles, t)

    @pl.when(i == 0)
    def _():
        ucarry[...] = jnp.zeros_like(ucarry)

    _interleave(glu_pieces(0, zraw0, ubuf0))

    def slab_pair_step(m, carry):
        n = 2 * m
        _interleave(dot_pieces(n, zraw0), conv_pieces(n - 2, ubuf0),
                    glu_pieces(n - 1, zraw1, ubuf1))
        _interleave(dot_pieces(n + 1, zraw1), conv_pieces(n - 1, ubuf1),
                    glu_pieces(n, zraw0, ubuf0))
        return carry
    lax.fori_loop(1, n_lb // 2, slab_pair_step, 0)

    rb = t // ROW_BLOCKS

    def ln_pieces(q):
        def piece(rc):
            rows = slice(q * rb + rc * NORM_ROWS, q * rb + (rc + 1) * NORM_ROWS)
            v = jnp.concatenate([cslab[n, rows, :] for n in range(n_lb)], axis=-1)
            mu = jnp.mean(v, axis=-1, keepdims=True)
            vc = v - mu
            var = jnp.mean(vc * vc, axis=-1, keepdims=True)
            y = vc * lax.rsqrt(var + EPS) * lng_ref[...] + lnb_ref[...]
            act[rows, :] = (y * _sigmoid(y)).astype(BF16)
        return [functools.partial(piece, rc) for rc in range(rb // NORM_ROWS)]

    def yb_pieces(q):
        def piece():
            rows = slice(q * rb, (q + 1) * rb)
            yb[rows, :] = _dot(act[rows, :], wco_ref[...])
        return [piece]

    def merge_pieces(q):
        def piece(rc):
            rows = slice(q * rb + rc * NORM_ROWS, q * rb + (rc + 1) * NORM_ROWS)
            ga = jnp.concatenate([sga[n, rows, :] for n in range(n_lb)], axis=-1)
            gb = jnp.concatenate([sgb[n, rows, :] for n in range(n_lb)], axis=-1)
            mbuf[rows, :] = (ga * ya[rows, :] + gb * yb[rows, :]).astype(BF16)
        return [functools.partial(piece, rc) for rc in range(rb // NORM_ROWS)]

    def out_pieces(q):
        def piece():
            rows = slice(q * rb, (q + 1) * rb)
            obuf[rows, :] = _dot(mbuf[rows, :], wout_ref[...])
        return [piece]

    def resid_pieces(q, n_pieces=4):
        def piece(lo, hi):
            for j in range(lo, hi):
                ot[j] = xt[j] + ga1 * obuf[j * SUBLANES:(j + 1) * SUBLANES, :]
        per = rb // SUBLANES // n_pieces
        lo0 = q * rb // SUBLANES
        return [functools.partial(piece, lo0 + k * per, lo0 + (k + 1) * per) for k in range(n_pieces)]

    _interleave([functools.partial(group_piece, g) for g in range(n_grp)]
                + [functools.partial(ya_piece, mh) for mh in range(2)],
                norm_pieces(xin.at[nxt % N_IN_SLOTS], nxt // n_tiles)
                + glu_pieces(n_lb - 1, zraw1, ubuf1))
    _interleave(first_stage_mxu(),
                conv_pieces(n_lb - 2, ubuf0) + conv_pieces(n_lb - 1, ubuf1) + ln_pieces(0)
                + pool_pieces(nxt % n_tiles == 0))

    assert ROW_BLOCKS == 2
    _interleave(yb_pieces(0), ln_pieces(1))
    _interleave(yb_pieces(1), merge_pieces(0))
    _interleave(out_pieces(0), merge_pieces(1))
    _interleave(out_pieces(1), resid_pieces(0))
    _interleave(resid_pieces(1))

    _tile_epilogue(o_hbm, xout, sem_out, step, n_steps, n_tiles, t)


def _const_spec(shape):
    nd = len(shape)
    return pl.BlockSpec(shape, lambda b, i: (0,) * nd, pipeline_mode=pl.Buffered(1))


def _mix_call(x, mod, g1, w_zp, w_slab, w_grp, b_grp, pscale, w_po, cw, cb, lng, lnb, w_co, w_out):
    bsz, seq, d = x.shape
    t = TOKENS_PER_TILE
    l = t // SUBLANES
    n_tiles = seq // t
    w_pool = w_po.shape[0]
    n_lb, conv_k, _ = cw.shape
    hp8 = max(POOL_WINDOWS) * SUBLANES
    hu8 = (conv_k - 1) * SUBLANES
    kern = functools.partial(_mix_kernel, d=d, w_pool=w_pool, conv_k=conv_k, t=t,
                             n_tiles=n_tiles, n_steps=bsz * n_tiles)
    weights = (g1, w_zp, w_slab, w_grp, b_grp, pscale, w_po, cw, cb, lng, lnb, w_co, w_out)
    return pl.pallas_call(
        kern,
        grid=(bsz, n_tiles),
        in_specs=[pl.BlockSpec(memory_space=pl.ANY),
                  pl.BlockSpec((None, N_MOD, d), lambda b, i: (b, 0, 0)),
                  _const_spec(mod.shape)]
                 + [_const_spec(w.shape) for w in weights],
        out_specs=pl.BlockSpec(memory_space=pl.ANY),
        out_shape=jax.ShapeDtypeStruct(x.shape, F32),
        scratch_shapes=[
            pltpu.VMEM((N_IN_SLOTS, l, SUBLANES, d), F32),
            pltpu.VMEM((2, l, SUBLANES, d), F32),
            pltpu.SemaphoreType.DMA((N_IN_SLOTS, SUBLANES)),
            pltpu.SemaphoreType.DMA((2, SUBLANES)),
            pltpu.VMEM((t, d), BF16),
            pltpu.VMEM((t, d), BF16),
            pltpu.VMEM((hp8 + t, w_pool), F32),
            pltpu.VMEM((t, 4 * LANES), F32),
            pltpu.VMEM((t, 4 * LANES), F32),
            pltpu.VMEM((hu8 + t, LANES), F32),
            pltpu.VMEM((hu8 + t, LANES), F32),
            pltpu.VMEM((n_lb, t, LANES), F32),
            pltpu.VMEM((n_lb, t, LANES), F32),
            pltpu.VMEM((n_lb, t, LANES), F32),
            pltpu.VMEM((t, w_pool), F32),
            pltpu.VMEM((t, w_pool), BF16),
            pltpu.VMEM((t, d), F32),
            pltpu.VMEM((t, d), F32),
            pltpu.VMEM((t, d), F32),
            pltpu.VMEM((t, d), BF16),
            pltpu.VMEM((hp8, w_pool), F32),
            pltpu.VMEM((n_lb, hu8, LANES), F32),
        ],
        compiler_params=pltpu.CompilerParams(
            dimension_semantics=("arbitrary", "arbitrary"),
            vmem_limit_bytes=VMEM_LIMIT_BYTES),
        name="token_mix",
    )(x, mod, mod, *weights)


def _ffn_kernel(x_hbm, mod_ref, modall_ref, g2_ref, wup_ref, fcw_ref, fcb_ref, wdn_ref, gf_ref,
                o_hbm,
                xin, xout, sem_in, sem_out,
                hbuf, upg0, upv0, upg1, upv1, fbuf0, fbuf1, acc, gcarry, vcarry,
                *, d, d_ff, ffn_k, t, n_tiles, n_steps):
    i = pl.program_id(1)
    step = pl.program_id(0) * n_tiles + i
    nxt = jnp.minimum(step + 1, n_steps - 1)
    l = t // SUBLANES
    fc = MXU_COLS
    hf8 = (ffn_k - 1) * SUBLANES
    n_rc = t // ROW_CHUNK
    n_chunks = d_ff // fc
    xt = xin.at[step % N_IN_SLOTS]
    ot = xout.at[step % 2]
    ga2 = mod_ref[5:6, :]

    def norm_pieces(x_tile, b_idx):
        shift = modall_ref[b_idx, 3:4, :]
        gain = g2_ref[...] * (1.0 + modall_ref[b_idx, 4:5, :])
        return _norm_pieces(x_tile, hbuf, gain, shift, l)

    ups = ((upg0, upv0), (upg1, upv1))
    fbufs = (fbuf0, fbuf1)

    def up_pieces(c):
        def piece(which, mh):
            rows = slice(mh * (t // 2), (mh + 1) * (t // 2))
            lo = which * d_ff + c * fc
            ups[c % 2][which][hf8 + mh * (t // 2):hf8 + (mh + 1) * (t // 2), :] = _dot(
                hbuf[rows, :], wup_ref[:, lo:lo + fc])
        return [functools.partial(piece, which, mh) for which in range(2) for mh in range(2)]

    def conv_rows(buf, r0, cols):
        y = fcw_ref[0:1, cols] * buf[r0:r0 + ROW_CHUNK, :] + fcb_ref[:, cols]
        for k in range(1, ffn_k):
            y = y + fcw_ref[k:k + 1, cols] * buf[r0 + k * SUBLANES:r0 + k * SUBLANES + ROW_CHUNK, :]
        return y

    def gate_pieces(c):
        bg, bv = ups[c % 2]
        gcols = slice(c * fc, (c + 1) * fc)
        vcols = slice(d_ff + c * fc, d_ff + (c + 1) * fc)

        def piece(rc):
            if rc == 0:
                for buf, carry in ((bg, gcarry), (bv, vcarry)):
                    tail = buf[t:hf8 + t, :]
                    buf[0:hf8, :] = _halo_groups(tail, carry[c])
                    carry[c] = tail
            r0 = rc * ROW_CHUNK
            yg = conv_rows(bg, r0, gcols)
            yv = conv_rows(bv, r0, vcols)
            fbufs[c % 2][r0:r0 + ROW_CHUNK, :] = (yg * _sigmoid(yg) * yv).astype(BF16)
        return [functools.partial(piece, rc) for rc in range(n_rc)]

    def down_pieces(c):
        def piece(nh):
            cols = slice(nh * (d // 2), (nh + 1) * (d // 2))
            down = _dot(fbufs[c % 2][...], wdn_ref[c * fc:(c + 1) * fc, cols])
            if c == 0:
                acc[:, cols] = down
            else:
                acc[:, cols] += down
        return [functools.partial(piece, nh) for nh in range(2)]

    def final_pieces(n_pieces=8):
        def piece(lo, hi):
            for j in range(lo, hi):
                x2 = xt[j] + ga2 * acc[j * SUBLANES:(j + 1) * SUBLANES, :]
                ms = jnp.mean(x2 * x2, axis=-1, keepdims=True)
                ot[j] = x2 * lax.rsqrt(ms + EPS) * gf_ref[...]
        per = l // n_pieces
        return [functools.partial(piece, q * per, (q + 1) * per) for q in range(n_pieces)]

    @pl.when(step == 0)
    def _():
        _first_tiles_load(x_hbm, xin, sem_in, n_steps, n_tiles, t)
        _interleave(norm_pieces(xin.at[0], 0))
        _interleave(up_pieces(0))

    _tile_prologue(x_hbm, o_hbm, xin, xout, sem_in, sem_out, step, n_steps, n_tiles, t)

    @pl.when(i == 0)
    def _():
        gcarry[...] = jnp.zeros_like(gcarry)
        vcarry[...] = jnp.zeros_like(vcarry)

    for c in range(n_chunks):
        mxu = up_pieces(c + 1) if c + 1 < n_chunks else []
        mxu += down_pieces(c - 1) if c >= 1 else []
        _interleave(mxu, gate_pieces(c))

    _interleave(down_pieces(n_chunks - 1),
                norm_pieces(xin.at[nxt % N_IN_SLOTS], nxt // n_tiles))
    _interleave(up_pieces(0), final_pieces())

    _tile_epilogue(o_hbm, xout, sem_out, step, n_steps, n_tiles, t)


def _ffn_call(x, mod, g2, w_up, fcw, fcb, w_dn, g_final):
    bsz, seq, d = x.shape
    t = TOKENS_PER_TILE
    l = t // SUBLANES
    n_tiles = seq // t
    d_ff = w_dn.shape[0]
    ffn_k = fcw.shape[0]
    hf8 = (ffn_k - 1) * SUBLANES
    n_chunks = d_ff // MXU_COLS
    kern = functools.partial(_ffn_kernel, d=d, d_ff=d_ff, ffn_k=ffn_k, t=t,
                             n_tiles=n_tiles, n_steps=bsz * n_tiles)
    weights = (g2, w_up, fcw, fcb, w_dn, g_final)
    return pl.pallas_call(
        kern,
        grid=(bsz, n_tiles),
        in_specs=[pl.BlockSpec(memory_space=pl.ANY),
                  pl.BlockSpec((None, N_MOD, d), lambda b, i: (b, 0, 0)),
                  _const_spec(mod.shape)]
                 + [_const_spec(w.shape) for w in weights],
        out_specs=pl.BlockSpec(memory_space=pl.ANY),
        out_shape=jax.ShapeDtypeStruct(x.shape, F32),
        scratch_shapes=[
            pltpu.VMEM((N_IN_SLOTS, l, SUBLANES, d), F32),
            pltpu.VMEM((2, l, SUBLANES, d), F32),
            pltpu.SemaphoreType.DMA((N_IN_SLOTS, SUBLANES)),
            pltpu.SemaphoreType.DMA((2, SUBLANES)),
            pltpu.VMEM((t, d), BF16),
            pltpu.VMEM((hf8 + t, MXU_COLS), F32),
            pltpu.VMEM((hf8 + t, MXU_COLS), F32),
            pltpu.VMEM((hf8 + t, MXU_COLS), F32),
            pltpu.VMEM((hf8 + t, MXU_COLS), F32),
            pltpu.VMEM((t, MXU_COLS), BF16),
            pltpu.VMEM((t, MXU_COLS), BF16),
            pltpu.VMEM((t, d), F32),
            pltpu.VMEM((n_chunks, hf8, MXU_COLS), F32),
            pltpu.VMEM((n_chunks, hf8, MXU_COLS), F32),
        ],
        compiler_params=pltpu.CompilerParams(
            dimension_semantics=("arbitrary", "arbitrary"),
            vmem_limit_bytes=VMEM_LIMIT_BYTES),
        name="conv_ffn",
    )(x, mod, mod, *weights)


def _slab_cols(a, n_parts):
    r, cols = a.shape
    dd = cols // n_parts
    a4 = a.reshape(r, n_parts, dd // LANES, LANES)
    return jnp.transpose(a4, (2, 0, 1, 3)).reshape(dd // LANES, r, n_parts * LANES)


def kernel(x, c, w_ada, b_ada, g_norm1, w_in, w_pool_grp, b_pool_grp, pool_scale, w_pool_out, conv_w, conv_b, ln_g, ln_b, w_conv_out, w_out, g_norm2, w_up, ffn_conv_w, ffn_conv_b, w_down, g_final):
    bsz, seq, d = x.shape
    depth = w_in.shape[0]
    t = TOKENS_PER_TILE
    d_ff = w_down.shape[1]
    w_pool = w_pool_out.shape[1]
    assert depth == 1, "the final RMSNorm is fused into the (single) ConvFFN call"
    assert seq % t == 0 and seq // t >= 2 and d_ff % MXU_COLS == 0 and d % LANES == 0
    assert t // SUBLANES >= conv_w.shape[1] - 1 and (d // LANES) % 2 == 0 and d // LANES >= 4

    for l in range(depth):
        mod = _ada_call(c, w_ada[l], b_ada[l]).reshape(bsz, N_MOD, d)
        w_in_bf = w_in[l].astype(BF16)
        x = _mix_call(
            x, mod, g_norm1[l][None], w_in_bf[:, :w_pool], _slab_cols(w_in_bf[:, w_pool:], 4),
            w_pool_grp[l].astype(BF16), b_pool_grp[l].reshape(1, -1), pool_scale[l][None],
            w_pool_out[l].astype(BF16), _slab_cols(conv_w[l], 1), _slab_cols(conv_b[l][None], 1),
            ln_g[l][None], ln_b[l][None], w_conv_out[l].astype(BF16), w_out[l].astype(BF16))
        x = _ffn_call(
            x, mod, g_norm2[l][None], w_up[l].astype(BF16), ffn_conv_w[l],
            ffn_conv_b[l][None], w_down[l].astype(BF16), g_final[None])
    return x
```

```python
import functools

import jax
import jax.numpy as jnp
from jax import lax
from jax.experimental import pallas as pl
from jax.experimental.pallas import tpu as pltpu

F32 = jnp.float32
BF16 = jnp.bfloat16

EPS = 1e-6
POOL_WINDOWS = (2, 4, 8, 16)
N_MOD = 6

SUBLANES = 8
LANES = 128
MXU_COLS = 256
TOKENS_PER_TILE = 512
ROW_CHUNK = 128
NORM_ROWS = 16
UP_ROWS = 128
ADA_COLS = 512
VMEM_LIMIT_BYTES = 60 * 1024 * 1024


def _dot(a, b):
    return jnp.dot(a, b, preferred_element_type=F32)


def _sigmoid(x):
    return jax.nn.sigmoid(x)


def _interleave(*piece_lists):
    keyed = []
    for li, pieces in enumerate(piece_lists):
        for k, piece in enumerate(pieces):
            keyed.append((k / len(pieces), li, k, piece))
    for _, _, _, piece in sorted(keyed, key=lambda e: e[:3]):
        piece()


def _halo_groups(cur_tail, prev_tail):
    n = cur_tail.shape[0] // SUBLANES
    sub = lax.broadcasted_iota(jnp.int32, cur_tail.shape, 0) % SUBLANES
    mixed = jnp.where(sub == SUBLANES - 1, prev_tail, cur_tail)
    groups = [pltpu.roll(mixed[g * SUBLANES:(g + 1) * SUBLANES], 1, 0) for g in range(n)]
    return jnp.concatenate(groups, axis=0) if n > 1 else groups[0]


def _tile_copies(hbm, buf, sem, step, n_tiles, t, to_vmem):
    l = t // SUBLANES
    b = step // n_tiles
    row0 = (step % n_tiles) * t
    slot = step % 2
    copies = []
    for s in range(SUBLANES):
        h = hbm.at[b, pl.ds(row0 + s * l, l), :]
        v = buf.at[slot, :, s, :]
        src, dst = (h, v) if to_vmem else (v, h)
        copies.append(pltpu.make_async_copy(src, dst, sem.at[slot, s]))
    return copies


def _start(copies):
    for cp in copies:
        cp.start()


def _wait(copies):
    for cp in copies:
        cp.wait()


def _tile_prologue(x_hbm, o_hbm, xin, xout, sem_in, sem_out, step, n_steps, n_tiles, t):
    @pl.when(step == 0)
    def _():
        _start(_tile_copies(x_hbm, xin, sem_in, step, n_tiles, t, True))

    @pl.when(step + 1 < n_steps)
    def _():
        _start(_tile_copies(x_hbm, xin, sem_in, step + 1, n_tiles, t, True))

    _wait(_tile_copies(x_hbm, xin, sem_in, step, n_tiles, t, True))

    @pl.when(step >= 2)
    def _():
        _wait(_tile_copies(o_hbm, xout, sem_out, step - 2, n_tiles, t, False))


def _tile_epilogue(o_hbm, xout, sem_out, step, n_steps, n_tiles, t):
    _start(_tile_copies(o_hbm, xout, sem_out, step, n_tiles, t, False))

    @pl.when(step == n_steps - 1)
    def _():
        if n_steps >= 2:
            _wait(_tile_copies(o_hbm, xout, sem_out, step - 1, n_tiles, t, False))
        _wait(_tile_copies(o_hbm, xout, sem_out, step, n_tiles, t, False))


def _norm_modulate_to_bf16(xt, hbuf, gain, shift, n_groups, first_group=0):
    for jp in range(first_group // 2, (first_group + n_groups) // 2):
        xx = jnp.concatenate([xt[2 * jp], xt[2 * jp + 1]], axis=0)
        ms = jnp.mean(xx * xx, axis=-1, keepdims=True)
        h = xx * lax.rsqrt(ms + EPS) * gain + shift
        hbuf[jp * 16:(jp + 1) * 16, :] = h.astype(BF16)


def _ada_kernel(c_ref, w_ref, b_ref, o_ref):
    c = c_ref[...]
    ca = c * _sigmoid(c)
    w = w_ref[...]
    ca_hi = ca.astype(BF16)
    ca_lo = (ca - ca_hi.astype(F32)).astype(BF16)
    w_hi = w.astype(BF16)
    w_lo = (w - w_hi.astype(F32)).astype(BF16)
    acc = _dot(ca_hi, w_hi) + _dot(ca_lo, w_hi) + _dot(ca_hi, w_lo)
    o_ref[...] = acc + b_ref[...]


def _ada_call(c, w_ada, b_ada):
    b, d = c.shape
    n = w_ada.shape[1]
    return pl.pallas_call(
        _ada_kernel,
        grid=(n // ADA_COLS,),
        in_specs=[
            pl.BlockSpec((b, d), lambda j: (0, 0)),
            pl.BlockSpec((d, ADA_COLS), lambda j: (0, j)),
            pl.BlockSpec((1, ADA_COLS), lambda j: (0, j)),
        ],
        out_specs=pl.BlockSpec((b, ADA_COLS), lambda j: (0, j)),
        out_shape=jax.ShapeDtypeStruct((b, n), F32),
        compiler_params=pltpu.CompilerParams(dimension_semantics=("arbitrary",)),
        name="adaln_mod",
    )(c, w_ada, b_ada.reshape(1, n))


def _mix_kernel(x_hbm, mod_ref, g1_ref, wzp_ref, wslab_ref, wgrp_ref, bgrp_ref, pscale_ref,
                wpo_ref, cw_ref, cb_ref, lng_ref, lnb_ref, wco_ref, wout_ref,
                o_hbm,
                xin, xout, sem_in, sem_out,
                hbuf, zpbuf, zraw0, zraw1, ubuf0, ubuf1, sga, sgb, cslab, invc, pbuf, ya, yb, obuf, act,
                zcarry, ucarry,
                *, d, w_pool, conv_k, t, n_tiles, n_steps):
    i = pl.program_id(1)
    step = pl.program_id(0) * n_tiles + i
    slot = step % 2
    l = t // SUBLANES
    n_grp = len(POOL_WINDOWS)
    pg = w_pool // n_grp
    hp8 = max(POOL_WINDOWS) * SUBLANES
    hu8 = (conv_k - 1) * SUBLANES
    n_rc = t // ROW_CHUNK
    n_lb = d // LANES

    _tile_prologue(x_hbm, o_hbm, xin, xout, sem_in, sem_out, step, n_steps, n_tiles, t)
    xt = xin.at[slot]
    ot = xout.at[slot]

    @pl.when(i == 0)
    def _():
        zcarry[...] = jnp.zeros_like(zcarry)
        ucarry[...] = jnp.zeros_like(ucarry)
        r = lax.broadcasted_iota(jnp.int32, (t, pg), 0)
        tok = (r % SUBLANES) * l + r // SUBLANES
        for g, w in enumerate(POOL_WINDOWS):
            cnt = jnp.minimum(tok + 1, w).astype(F32)
            invc[:, g * pg:(g + 1) * pg] = 1.0 / cnt

    @pl.when(i == 1)
    def _():
        for g, w in enumerate(POOL_WINDOWS):
            invc[:, g * pg:(g + 1) * pg] = jnp.full((t, pg), 1.0 / w, F32)

    sh1 = mod_ref[0:1, :]
    gain1 = g1_ref[...] * (1.0 + mod_ref[1:2, :])
    ga1 = mod_ref[2:3, :]

    def dot_pieces(n, zraw):
        def piece():
            zraw[...] = _dot(hbuf[...], wslab_ref[n])
        return [piece]

    def glu_pieces(n, zraw, ubuf):
        def piece(rc):
            r0 = rc * ROW_CHUNK
            blk = zraw[r0:r0 + ROW_CHUNK, :]
            ubuf[hu8 + r0:hu8 + r0 + ROW_CHUNK, :] = (
                blk[:, 0:LANES] * _sigmoid(blk[:, LANES:2 * LANES]))
            sga[n, r0:r0 + ROW_CHUNK, :] = _sigmoid(blk[:, 2 * LANES:3 * LANES])
            sgb[n, r0:r0 + ROW_CHUNK, :] = _sigmoid(blk[:, 3 * LANES:4 * LANES])
        return [functools.partial(piece, rc) for rc in range(n_rc)]

    def conv_pieces(n, ubuf):
        def piece(rc):
            if rc == 0:
                tail = ubuf[t:hu8 + t, :]
                ubuf[0:hu8, :] = _halo_groups(tail, ucarry[n])
                ucarry[n] = tail
            r0 = rc * ROW_CHUNK
            n_out = ROW_CHUNK // SUBLANES
            win = [ubuf[r0 + g * SUBLANES:r0 + (g + 1) * SUBLANES, :]
                   for g in range(n_out + conv_k - 1)]
            bias = jnp.broadcast_to(cb_ref[n], (SUBLANES, LANES))
            accs = [bias] * n_out
            for k0 in range(0, conv_k, 4):
                ks = range(k0, min(k0 + 4, conv_k))
                wks = [jnp.broadcast_to(cw_ref[n, k:k + 1, :], (SUBLANES, LANES)) for k in ks]
                for j in range(n_out):
                    prods = [wk * win[j + k] for wk, k in zip(wks, ks)]
                    while len(prods) > 1:
                        prods = [a + b for a, b in zip(prods[::2], prods[1::2])] + (
                            [prods[-1]] if len(prods) % 2 else [])
                    accs[j] = accs[j] + prods[0]
            cslab[n, r0:r0 + ROW_CHUNK, :] = jnp.concatenate(accs, axis=0)
        return [functools.partial(piece, rc) for rc in range(n_rc)]

    def pool_stage():
        z_tail = zpbuf[t:hp8 + t, :]
        zpbuf[0:hp8, :] = _halo_groups(z_tail, zcarry[...])
        zcarry[...] = z_tail
        for rc in range(n_rc):
            r0 = rc * ROW_CHUNK
            for g, w in enumerate(POOL_WINDOWS):
                cols = slice(g * pg, (g + 1) * pg)
                zc = zpbuf[hp8 + r0:hp8 + r0 + ROW_CHUNK, cols]
                s = zc
                for k in range(1, w):
                    s = s + zpbuf[hp8 + r0 - k * SUBLANES:hp8 + r0 - k * SUBLANES + ROW_CHUNK, cols]
                pooled = s * invc[r0:r0 + ROW_CHUNK, cols] - zc
                pbuf[r0:r0 + ROW_CHUNK, cols] = pooled.astype(BF16)

    def zp_piece():
        zpbuf[hp8:hp8 + t, :] = _dot(hbuf[...], wzp_ref[...])

    def group_piece(g):
        cols = slice(g * pg, (g + 1) * pg)
        pa = _dot(pbuf[:, cols], wgrp_ref[g]) + bgrp_ref[:, cols]
        pbuf[:, cols] = (pa * pscale_ref[:, cols]).astype(BF16)

    def ya_piece():
        ya[...] = _dot(pbuf[...], wpo_ref[...])

    _norm_modulate_to_bf16(xt, hbuf, gain1, sh1, l)
    zp_piece()
    _interleave(dot_pieces(0, zraw0), [pool_stage])
    _interleave(dot_pieces(1, zraw1), glu_pieces(0, zraw0, ubuf0))

    def slab_pair_step(m, carry):
        n = 2 * m
        _interleave(dot_pieces(n, zraw0), conv_pieces(n - 2, ubuf0),
                    glu_pieces(n - 1, zraw1, ubuf1))
        _interleave(dot_pieces(n + 1, zraw1), conv_pieces(n - 1, ubuf1),
                    glu_pieces(n, zraw0, ubuf0))
        return carry
    lax.fori_loop(1, n_lb // 2, slab_pair_step, 0)

    _interleave([functools.partial(group_piece, g) for g in range(n_grp)],
                conv_pieces(n_lb - 2, ubuf0), glu_pieces(n_lb - 1, zraw1, ubuf1))
    _interleave([ya_piece], conv_pieces(n_lb - 1, ubuf1))

    for rc in range(t // NORM_ROWS):
        rows = slice(rc * NORM_ROWS, (rc + 1) * NORM_ROWS)
        v = jnp.concatenate([cslab[n, rows, :] for n in range(n_lb)], axis=-1)
        mu = jnp.mean(v, axis=-1, keepdims=True)
        vc = v - mu
        var = jnp.mean(vc * vc, axis=-1, keepdims=True)
        y = vc * lax.rsqrt(var + EPS) * lng_ref[...] + lnb_ref[...]
        act[rows, :] = (y * _sigmoid(y)).astype(BF16)

    yb[...] = _dot(act[...], wco_ref[...])

    for rc in range(t // NORM_ROWS):
        rows = slice(rc * NORM_ROWS, (rc + 1) * NORM_ROWS)
        ga = jnp.concatenate([sga[n, rows, :] for n in range(n_lb)], axis=-1)
        gb = jnp.concatenate([sgb[n, rows, :] for n in range(n_lb)], axis=-1)
        hbuf[rows, :] = (ga * ya[rows, :] + gb * yb[rows, :]).astype(BF16)

    obuf[...] = _dot(hbuf[...], wout_ref[...])

    for j in range(l):
        ot[j] = xt[j] + ga1 * obuf[j * SUBLANES:(j + 1) * SUBLANES, :]

    _tile_epilogue(o_hbm, xout, sem_out, step, n_steps, n_tiles, t)


def _const_spec(shape):
    nd = len(shape)
    return pl.BlockSpec(shape, lambda b, i: (0,) * nd, pipeline_mode=pl.Buffered(1))


def _mix_call(x, mod, g1, w_zp, w_slab, w_grp, b_grp, pscale, w_po, cw, cb, lng, lnb, w_co, w_out):
    bsz, seq, d = x.shape
    t = TOKENS_PER_TILE
    l = t // SUBLANES
    n_tiles = seq // t
    w_pool = w_po.shape[0]
    n_lb, conv_k, _ = cw.shape
    hp8 = max(POOL_WINDOWS) * SUBLANES
    hu8 = (conv_k - 1) * SUBLANES
    kern = functools.partial(_mix_kernel, d=d, w_pool=w_pool, conv_k=conv_k, t=t,
                             n_tiles=n_tiles, n_steps=bsz * n_tiles)
    weights = (g1, w_zp, w_slab, w_grp, b_grp, pscale, w_po, cw, cb, lng, lnb, w_co, w_out)
    return pl.pallas_call(
        kern,
        grid=(bsz, n_tiles),
        in_specs=[pl.BlockSpec(memory_space=pl.ANY),
                  pl.BlockSpec((None, N_MOD, d), lambda b, i: (b, 0, 0))]
                 + [_const_spec(w.shape) for w in weights],
        out_specs=pl.BlockSpec(memory_space=pl.ANY),
        out_shape=jax.ShapeDtypeStruct(x.shape, F32),
        scratch_shapes=[
            pltpu.VMEM((2, l, SUBLANES, d), F32),
            pltpu.VMEM((2, l, SUBLANES, d), F32),
            pltpu.SemaphoreType.DMA((2, SUBLANES)),
            pltpu.SemaphoreType.DMA((2, SUBLANES)),
            pltpu.VMEM((t, d), BF16),
            pltpu.VMEM((hp8 + t, w_pool), F32),
            pltpu.VMEM((t, 4 * LANES), F32),
            pltpu.VMEM((t, 4 * LANES), F32),
            pltpu.VMEM((hu8 + t, LANES), F32),
            pltpu.VMEM((hu8 + t, LANES), F32),
            pltpu.VMEM((n_lb, t, LANES), F32),
            pltpu.VMEM((n_lb, t, LANES), F32),
            pltpu.VMEM((n_lb, t, LANES), F32),
            pltpu.VMEM((t, w_pool), F32),
            pltpu.VMEM((t, w_pool), BF16),
            pltpu.VMEM((t, d), F32),
            pltpu.VMEM((t, d), F32),
            pltpu.VMEM((t, d), F32),
            pltpu.VMEM((t, d), BF16),
            pltpu.VMEM((hp8, w_pool), F32),
            pltpu.VMEM((n_lb, hu8, LANES), F32),
        ],
        compiler_params=pltpu.CompilerParams(
            dimension_semantics=("arbitrary", "arbitrary"),
            vmem_limit_bytes=VMEM_LIMIT_BYTES),
        name="token_mix",
    )(x, mod, *weights)


def _ffn_kernel(x_hbm, mod_ref, g2_ref, wup_ref, fcw_ref, fcb_ref, wdn_ref, gf_ref,
                o_hbm,
                xin, xout, sem_in, sem_out,
                hbuf, upg0, upv0, upg1, upv1, fbuf0, fbuf1, acc, gcarry, vcarry,
                *, d, d_ff, ffn_k, t, n_tiles, n_steps):
    i = pl.program_id(1)
    step = pl.program_id(0) * n_tiles + i
    slot = step % 2
    l = t // SUBLANES
    fc = MXU_COLS
    hf8 = (ffn_k - 1) * SUBLANES
    n_rc = t // ROW_CHUNK
    n_chunks = d_ff // fc
    n_mb = t // UP_ROWS

    _tile_prologue(x_hbm, o_hbm, xin, xout, sem_in, sem_out, step, n_steps, n_tiles, t)
    xt = xin.at[slot]
    ot = xout.at[slot]

    @pl.when(i == 0)
    def _():
        gcarry[...] = jnp.zeros_like(gcarry)
        vcarry[...] = jnp.zeros_like(vcarry)

    sh2 = mod_ref[3:4, :]
    gain2 = g2_ref[...] * (1.0 + mod_ref[4:5, :])
    ga2 = mod_ref[5:6, :]

    ups = ((upg0, upv0), (upg1, upv1))
    fbufs = (fbuf0, fbuf1)

    def up_pieces(c):
        def block(which, mb):
            lo = which * d_ff + c * fc
            rows = slice(mb * UP_ROWS, (mb + 1) * UP_ROWS)
            ups[c % 2][which][hf8 + mb * UP_ROWS:hf8 + (mb + 1) * UP_ROWS, :] = _dot(
                hbuf[rows, :], wup_ref[:, lo:lo + fc])
        return [functools.partial(block, which, mb) for which in range(2) for mb in range(n_mb)]

    def conv_rows(buf, r0, cols):
        y = fcw_ref[0:1, cols] * buf[r0:r0 + ROW_CHUNK, :] + fcb_ref[:, cols]
        for k in range(1, ffn_k):
            y = y + fcw_ref[k:k + 1, cols] * buf[r0 + k * SUBLANES:r0 + k * SUBLANES + ROW_CHUNK, :]
        return y

    def gate_pieces(c):
        bg, bv = ups[c % 2]
        gcols = slice(c * fc, (c + 1) * fc)
        vcols = slice(d_ff + c * fc, d_ff + (c + 1) * fc)

        def piece(rc):
            if rc == 0:
                for buf, carry in ((bg, gcarry), (bv, vcarry)):
                    tail = buf[t:hf8 + t, :]
                    buf[0:hf8, :] = _halo_groups(tail, carry[c])
                    carry[c] = tail
            r0 = rc * ROW_CHUNK
            yg = conv_rows(bg, r0, gcols)
            yv = conv_rows(bv, r0, vcols)
            fbufs[c % 2][r0:r0 + ROW_CHUNK, :] = (yg * _sigmoid(yg) * yv).astype(BF16)
        return [functools.partial(piece, rc) for rc in range(n_rc)]

    def down_pieces(c):
        def piece(nq):
            cols = slice(nq * MXU_COLS, (nq + 1) * MXU_COLS)
            down = _dot(fbufs[c % 2][...], wdn_ref[c * fc:(c + 1) * fc, cols])
            if c == 0:
                acc[:, cols] = down
            else:
                acc[:, cols] += down
        return [functools.partial(piece, nq) for nq in range(d // MXU_COLS)]

    up0 = up_pieces(0)
    gpb = UP_ROWS // SUBLANES
    _norm_modulate_to_bf16(xt, hbuf, gain2, sh2, gpb, 0)
    for mb in range(n_mb):
        up0[mb]()
        up0[n_mb + mb]()
        if mb + 1 < n_mb:
            _norm_modulate_to_bf16(xt, hbuf, gain2, sh2, gpb, (mb + 1) * gpb)

    for c in range(n_chunks):
        mxu = up_pieces(c + 1) if c + 1 < n_chunks else []
        mxu += down_pieces(c - 1) if c >= 1 else []
        _interleave(mxu, gate_pieces(c))
    _interleave(down_pieces(n_chunks - 1))

    for j in range(l):
        x2 = xt[j] + ga2 * acc[j * SUBLANES:(j + 1) * SUBLANES, :]
        ms = jnp.mean(x2 * x2, axis=-1, keepdims=True)
        ot[j] = x2 * lax.rsqrt(ms + EPS) * gf_ref[...]

    _tile_epilogue(o_hbm, xout, sem_out, step, n_steps, n_tiles, t)


def _ffn_call(x, mod, g2, w_up, fcw, fcb, w_dn, g_final):
    bsz, seq, d = x.shape
    t = TOKENS_PER_TILE
    l = t // SUBLANES
    n_tiles = seq // t
    d_ff = w_dn.shape[0]
    ffn_k = fcw.shape[0]
    hf8 = (ffn_k - 1) * SUBLANES
    n_chunks = d_ff // MXU_COLS
    kern = functools.partial(_ffn_kernel, d=d, d_ff=d_ff, ffn_k=ffn_k, t=t,
                             n_tiles=n_tiles, n_steps=bsz * n_tiles)
    weights = (g2, w_up, fcw, fcb, w_dn, g_final)
    return pl.pallas_call(
        kern,
        grid=(bsz, n_tiles),
        in_specs=[pl.BlockSpec(memory_space=pl.ANY),
                  pl.BlockSpec((None, N_MOD, d), lambda b, i: (b, 0, 0))]
                 + [_const_spec(w.shape) for w in weights],
        out_specs=pl.BlockSpec(memory_space=pl.ANY),
        out_shape=jax.ShapeDtypeStruct(x.shape, F32),
        scratch_shapes=[
            pltpu.VMEM((2, l, SUBLANES, d), F32),
            pltpu.VMEM((2, l, SUBLANES, d), F32),
            pltpu.SemaphoreType.DMA((2, SUBLANES)),
            pltpu.SemaphoreType.DMA((2, SUBLANES)),
            pltpu.VMEM((t, d), BF16),
            pltpu.VMEM((hf8 + t, MXU_COLS), F32),
            pltpu.VMEM((hf8 + t, MXU_COLS), F32),
            pltpu.VMEM((hf8 + t, MXU_COLS), F32),
            pltpu.VMEM((hf8 + t, MXU_COLS), F32),
            pltpu.VMEM((t, MXU_COLS), BF16),
            pltpu.VMEM((t, MXU_COLS), BF16),
            pltpu.VMEM((t, d), F32),
            pltpu.VMEM((n_chunks, hf8, MXU_COLS), F32),
            pltpu.VMEM((n_chunks, hf8, MXU_COLS), F32),
        ],
        compiler_params=pltpu.CompilerParams(
            dimension_semantics=("arbitrary", "arbitrary"),
            vmem_limit_bytes=VMEM_LIMIT_BYTES),
        name="conv_ffn",
    )(x, mod, *weights)


def _slab_cols(a, n_parts):
    r, cols = a.shape
    dd = cols // n_parts
    a4 = a.reshape(r, n_parts, dd // LANES, LANES)
    return jnp.transpose(a4, (2, 0, 1, 3)).reshape(dd // LANES, r, n_parts * LANES)


def kernel(x, c, w_ada, b_ada, g_norm1, w_in, w_pool_grp, b_pool_grp, pool_scale, w_pool_out, conv_w, conv_b, ln_g, ln_b, w_conv_out, w_out, g_norm2, w_up, ffn_conv_w, ffn_conv_b, w_down, g_final):
    bsz, seq, d = x.shape
    depth = w_in.shape[0]
    t = TOKENS_PER_TILE
    d_ff = w_down.shape[1]
    w_pool = w_pool_out.shape[1]
    assert depth == 1, "the final RMSNorm is fused into the (single) ConvFFN call"
    assert seq % t == 0 and d_ff % MXU_COLS == 0 and d % MXU_COLS == 0
    assert t // SUBLANES >= conv_w.shape[1] - 1 and (d // LANES) % 2 == 0 and d // LANES >= 4
    assert all(t % n == 0 for n in (UP_ROWS, ROW_CHUNK, NORM_ROWS))

    for l in range(depth):
        mod = _ada_call(c, w_ada[l], b_ada[l]).reshape(bsz, N_MOD, d)
        w_in_bf = w_in[l].astype(BF16)
        x = _mix_call(
            x, mod, g_norm1[l][None], w_in_bf[:, :w_pool], _slab_cols(w_in_bf[:, w_pool:], 4),
            w_pool_grp[l].astype(BF16), b_pool_grp[l].reshape(1, -1), pool_scale[l][None],
            w_pool_out[l].astype(BF16), _slab_cols(conv_w[l], 1), _slab_cols(conv_b[l][None], 1),
            ln_g[l][None], ln_b[l][None], w_conv_out[l].astype(BF16), w_out[l].astype(BF16))
        x = _ffn_call(
            x, mod, g_norm2[l][None], w_up[l].astype(BF16), ffn_conv_w[l],
            ffn_conv_b[l][None], w_down[l].astype(BF16), g_final[None])
    return x
```

```python
import functools

import jax
import jax.numpy as jnp
from jax import lax
from jax.experimental import pallas as pl
from jax.experimental.pallas import tpu as pltpu

F32 = jnp.float32
BF16 = jnp.bfloat16

EPS = 1e-6
POOL_WINDOWS = (2, 4, 8, 16)
N_MOD = 6

SUBLANES = 8
LANES = 128
MXU_COLS = 256
TOKENS_PER_TILE = 512
ROW_CHUNK = 128
NORM_ROWS = 16
UP_ROWS = 128
CHAIN_ROWS = 512
ADA_COLS = 512
VMEM_LIMIT_BYTES = 60 * 1024 * 1024


def _dot(a, b):
    return jnp.dot(a, b, preferred_element_type=F32)


def _sigmoid(x):
    return jax.nn.sigmoid(x)


def _interleave(*piece_lists):
    keyed = []
    for li, pieces in enumerate(piece_lists):
        for k, piece in enumerate(pieces):
            keyed.append((k / len(pieces), li, k, piece))
    for _, _, _, piece in sorted(keyed, key=lambda e: e[:3]):
        piece()


def _halo_groups(cur_tail, prev_tail):
    n = cur_tail.shape[0] // SUBLANES
    sub = lax.broadcasted_iota(jnp.int32, cur_tail.shape, 0) % SUBLANES
    mixed = jnp.where(sub == SUBLANES - 1, prev_tail, cur_tail)
    groups = [pltpu.roll(mixed[g * SUBLANES:(g + 1) * SUBLANES], 1, 0) for g in range(n)]
    return jnp.concatenate(groups, axis=0) if n > 1 else groups[0]


def _tile_copies(hbm, buf, sem, step, n_tiles, t, to_vmem):
    l = t // SUBLANES
    b = step // n_tiles
    row0 = (step % n_tiles) * t
    slot = step % 2
    copies = []
    for s in range(SUBLANES):
        h = hbm.at[b, pl.ds(row0 + s * l, l), :]
        v = buf.at[slot, :, s, :]
        src, dst = (h, v) if to_vmem else (v, h)
        copies.append(pltpu.make_async_copy(src, dst, sem.at[slot, s]))
    return copies


def _start(copies):
    for cp in copies:
        cp.start()


def _wait(copies):
    for cp in copies:
        cp.wait()


def _tile_in_prologue(x_hbm, xin, sem_in, step, n_steps, n_tiles, t):
    @pl.when(step == 0)
    def _():
        _start(_tile_copies(x_hbm, xin, sem_in, step, n_tiles, t, True))

    @pl.when(step + 1 < n_steps)
    def _():
        _start(_tile_copies(x_hbm, xin, sem_in, step + 1, n_tiles, t, True))

    _wait(_tile_copies(x_hbm, xin, sem_in, step, n_tiles, t, True))


def _tile_prologue(x_hbm, o_hbm, xin, xout, sem_in, sem_out, step, n_steps, n_tiles, t):
    _tile_in_prologue(x_hbm, xin, sem_in, step, n_steps, n_tiles, t)

    @pl.when(step >= 2)
    def _():
        _wait(_tile_copies(o_hbm, xout, sem_out, step - 2, n_tiles, t, False))


def _tile_epilogue(o_hbm, xout, sem_out, step, n_steps, n_tiles, t):
    _start(_tile_copies(o_hbm, xout, sem_out, step, n_tiles, t, False))

    @pl.when(step == n_steps - 1)
    def _():
        if n_steps >= 2:
            _wait(_tile_copies(o_hbm, xout, sem_out, step - 1, n_tiles, t, False))
        _wait(_tile_copies(o_hbm, xout, sem_out, step, n_tiles, t, False))


def _norm_modulate_to_bf16(xt, hbuf, gain, shift, n_groups, first_group=0):
    for jp in range(first_group // 2, (first_group + n_groups) // 2):
        xx = jnp.concatenate([xt[2 * jp], xt[2 * jp + 1]], axis=0)
        ms = jnp.mean(xx * xx, axis=-1, keepdims=True)
        h = xx * lax.rsqrt(ms + EPS) * gain + shift
        hbuf[jp * 16:(jp + 1) * 16, :] = h.astype(BF16)


def _ada_kernel(c_ref, w_ref, b_ref, o_ref):
    c = c_ref[...]
    ca = c * _sigmoid(c)
    w = w_ref[...]
    ca_hi = ca.astype(BF16)
    ca_lo = (ca - ca_hi.astype(F32)).astype(BF16)
    w_hi = w.astype(BF16)
    w_lo = (w - w_hi.astype(F32)).astype(BF16)
    acc = _dot(ca_hi, w_hi) + _dot(ca_lo, w_hi) + _dot(ca_hi, w_lo)
    o_ref[...] = acc + b_ref[...]


def _ada_call(c, w_ada, b_ada):
    b, d = c.shape
    n = w_ada.shape[1]
    return pl.pallas_call(
        _ada_kernel,
        grid=(n // ADA_COLS,),
        in_specs=[
            pl.BlockSpec((b, d), lambda j: (0, 0)),
            pl.BlockSpec((d, ADA_COLS), lambda j: (0, j)),
            pl.BlockSpec((1, ADA_COLS), lambda j: (0, j)),
        ],
        out_specs=pl.BlockSpec((b, ADA_COLS), lambda j: (0, j)),
        out_shape=jax.ShapeDtypeStruct((b, n), F32),
        compiler_params=pltpu.CompilerParams(dimension_semantics=("arbitrary",)),
        name="adaln_mod",
    )(c, w_ada, b_ada.reshape(1, n))


def _mix_kernel(x_hbm, mod_ref, g1_ref, wzp_ref, wslab_ref, wgrp_ref, bgrp_ref, pscale_ref,
                wpo_ref, cw_ref, cb_ref,
                cslab, sgb, gya_ref,
                xin, sem_in,
                hbuf, zpbuf, zraw0, zraw1, ubuf0, ubuf1, sga, invc, pbuf, ya,
                zcarry, ucarry,
                *, d, w_pool, conv_k, t, n_tiles, n_steps):
    i = pl.program_id(1)
    step = pl.program_id(0) * n_tiles + i
    slot = step % 2
    l = t // SUBLANES
    n_grp = len(POOL_WINDOWS)
    pg = w_pool // n_grp
    hp8 = max(POOL_WINDOWS) * SUBLANES
    hu8 = (conv_k - 1) * SUBLANES
    n_rc = t // ROW_CHUNK
    n_lb = d // LANES

    _tile_in_prologue(x_hbm, xin, sem_in, step, n_steps, n_tiles, t)
    xt = xin.at[slot]

    @pl.when(i == 0)
    def _():
        zcarry[...] = jnp.zeros_like(zcarry)
        ucarry[...] = jnp.zeros_like(ucarry)
        r = lax.broadcasted_iota(jnp.int32, (t, pg), 0)
        tok = (r % SUBLANES) * l + r // SUBLANES
        for g, w in enumerate(POOL_WINDOWS):
            cnt = jnp.minimum(tok + 1, w).astype(F32)
            invc[:, g * pg:(g + 1) * pg] = 1.0 / cnt

    @pl.when(i == 1)
    def _():
        for g, w in enumerate(POOL_WINDOWS):
            invc[:, g * pg:(g + 1) * pg] = jnp.full((t, pg), 1.0 / w, F32)

    sh1 = mod_ref[0:1, :]
    gain1 = g1_ref[...] * (1.0 + mod_ref[1:2, :])

    def dot_pieces(n, zraw):
        def piece():
            zraw[...] = _dot(hbuf[...], wslab_ref[n])
        return [piece]

    def glu_pieces(n, zraw, ubuf):
        def piece(rc):
            r0 = rc * ROW_CHUNK
            blk = zraw[r0:r0 + ROW_CHUNK, :]
            ubuf[hu8 + r0:hu8 + r0 + ROW_CHUNK, :] = (
                blk[:, 0:LANES] * _sigmoid(blk[:, LANES:2 * LANES]))
            sga[n, r0:r0 + ROW_CHUNK, :] = _sigmoid(blk[:, 2 * LANES:3 * LANES])
            sgb[n, r0:r0 + ROW_CHUNK, :] = _sigmoid(blk[:, 3 * LANES:4 * LANES])
        return [functools.partial(piece, rc) for rc in range(n_rc)]

    def conv_pieces(n, ubuf):
        def piece(rc):
            if rc == 0:
                tail = ubuf[t:hu8 + t, :]
                ubuf[0:hu8, :] = _halo_groups(tail, ucarry[n])
                ucarry[n] = tail
            r0 = rc * ROW_CHUNK
            n_out = ROW_CHUNK // SUBLANES
            win = [ubuf[r0 + g * SUBLANES:r0 + (g + 1) * SUBLANES, :]
                   for g in range(n_out + conv_k - 1)]
            bias = jnp.broadcast_to(cb_ref[n], (SUBLANES, LANES))
            accs = [bias] * n_out
            for k0 in range(0, conv_k, 4):
                ks = range(k0, min(k0 + 4, conv_k))
                wks = [jnp.broadcast_to(cw_ref[n, k:k + 1, :], (SUBLANES, LANES)) for k in ks]
                for j in range(n_out):
                    prods = [wk * win[j + k] for wk, k in zip(wks, ks)]
                    while len(prods) > 1:
                        prods = [a + b for a, b in zip(prods[::2], prods[1::2])] + (
                            [prods[-1]] if len(prods) % 2 else [])
                    accs[j] = accs[j] + prods[0]
            cslab[n, r0:r0 + ROW_CHUNK, :] = jnp.concatenate(accs, axis=0)
        return [functools.partial(piece, rc) for rc in range(n_rc)]

    def pool_stage():
        z_tail = zpbuf[t:hp8 + t, :]
        zpbuf[0:hp8, :] = _halo_groups(z_tail, zcarry[...])
        zcarry[...] = z_tail
        for rc in range(n_rc):
            r0 = rc * ROW_CHUNK
            for g, w in enumerate(POOL_WINDOWS):
                cols = slice(g * pg, (g + 1) * pg)
                zc = zpbuf[hp8 + r0:hp8 + r0 + ROW_CHUNK, cols]
                s = zc
                for k in range(1, w):
                    s = s + zpbuf[hp8 + r0 - k * SUBLANES:hp8 + r0 - k * SUBLANES + ROW_CHUNK, cols]
                pooled = s * invc[r0:r0 + ROW_CHUNK, cols] - zc
                pbuf[r0:r0 + ROW_CHUNK, cols] = pooled.astype(BF16)

    def zp_piece():
        zpbuf[hp8:hp8 + t, :] = _dot(hbuf[...], wzp_ref[...])

    def group_piece(g):
        cols = slice(g * pg, (g + 1) * pg)
        pa = _dot(pbuf[:, cols], wgrp_ref[g]) + bgrp_ref[:, cols]
        pbuf[:, cols] = (pa * pscale_ref[:, cols]).astype(BF16)

    def ya_piece():
        ya[...] = _dot(pbuf[...], wpo_ref[...])

    _norm_modulate_to_bf16(xt, hbuf, gain1, sh1, l)
    zp_piece()
    _interleave(dot_pieces(0, zraw0), [pool_stage])
    _interleave(dot_pieces(1, zraw1), glu_pieces(0, zraw0, ubuf0))

    def slab_pair_step(m, carry):
        n = 2 * m
        _interleave(dot_pieces(n, zraw0), conv_pieces(n - 2, ubuf0),
                    glu_pieces(n - 1, zraw1, ubuf1))
        _interleave(dot_pieces(n + 1, zraw1), conv_pieces(n - 1, ubuf1),
                    glu_pieces(n, zraw0, ubuf0))
        return carry
    lax.fori_loop(1, n_lb // 2, slab_pair_step, 0)

    _interleave([functools.partial(group_piece, g) for g in range(n_grp)],
                conv_pieces(n_lb - 2, ubuf0), glu_pieces(n_lb - 1, zraw1, ubuf1))
    _interleave([ya_piece], conv_pieces(n_lb - 1, ubuf1))

    for rc in range(t // NORM_ROWS):
        rows = slice(rc * NORM_ROWS, (rc + 1) * NORM_ROWS)
        ga = jnp.concatenate([sga[n, rows, :] for n in range(n_lb)], axis=-1)
        gya_ref[rows, :] = ga * ya[rows, :]


def _const_spec(shape):
    nd = len(shape)
    return pl.BlockSpec(shape, lambda b, i: (0,) * nd, pipeline_mode=pl.Buffered(1))


def _slab_spec(n_lb, t):
    return pl.BlockSpec((None, None, n_lb, t, LANES), lambda b, i: (b, i, 0, 0, 0))


def _mix_call(x, mod, g1, w_zp, w_slab, w_grp, b_grp, pscale, w_po, cw, cb):
    bsz, seq, d = x.shape
    t = TOKENS_PER_TILE
    l = t // SUBLANES
    n_tiles = seq // t
    w_pool = w_po.shape[0]
    n_lb, conv_k, _ = cw.shape
    hp8 = max(POOL_WINDOWS) * SUBLANES
    hu8 = (conv_k - 1) * SUBLANES
    kern = functools.partial(_mix_kernel, d=d, w_pool=w_pool, conv_k=conv_k, t=t,
                             n_tiles=n_tiles, n_steps=bsz * n_tiles)
    weights = (g1, w_zp, w_slab, w_grp, b_grp, pscale, w_po, cw, cb)
    slab_shape = jax.ShapeDtypeStruct((bsz, n_tiles, n_lb, t, LANES), F32)
    return pl.pallas_call(
        kern,
        grid=(bsz, n_tiles),
        in_specs=[pl.BlockSpec(memory_space=pl.ANY),
                  pl.BlockSpec((None, N_MOD, d), lambda b, i: (b, 0, 0))]
                 + [_const_spec(w.shape) for w in weights],
        out_specs=[_slab_spec(n_lb, t), _slab_spec(n_lb, t),
                   pl.BlockSpec((None, t, d), lambda b, i: (b, i, 0))],
        out_shape=[slab_shape, slab_shape, jax.ShapeDtypeStruct(x.shape, F32)],
        scratch_shapes=[
            pltpu.VMEM((2, l, SUBLANES, d), F32),
            pltpu.SemaphoreType.DMA((2, SUBLANES)),
            pltpu.VMEM((t, d), BF16),
            pltpu.VMEM((hp8 + t, w_pool), F32),
            pltpu.VMEM((t, 4 * LANES), F32),
            pltpu.VMEM((t, 4 * LANES), F32),
            pltpu.VMEM((hu8 + t, LANES), F32),
            pltpu.VMEM((hu8 + t, LANES), F32),
            pltpu.VMEM((n_lb, t, LANES), F32),
            pltpu.VMEM((t, w_pool), F32),
            pltpu.VMEM((t, w_pool), BF16),
            pltpu.VMEM((t, d), F32),
            pltpu.VMEM((hp8, w_pool), F32),
            pltpu.VMEM((n_lb, hu8, LANES), F32),
        ],
        compiler_params=pltpu.CompilerParams(
            dimension_semantics=("arbitrary", "arbitrary"),
            vmem_limit_bytes=VMEM_LIMIT_BYTES),
        name="token_mix",
    )(x, mod, *weights)


def _ffn_kernel(x_hbm, cs_ref, sgb_ref, gya_ref, mod_ref, lng_ref, lnb_ref, wco_ref, wout_ref,
                g2_ref, wup_ref, fcw_ref, fcb_ref, wdn_ref, gf_ref,
                o_hbm,
                xin, xout, sem_in, sem_out,
                act, mbuf, ybuf, x1buf, hbuf, upg0, upv0, upg1, upv1, fbuf0, fbuf1, acc, gcarry, vcarry,
                *, d, d_ff, ffn_k, t, n_tiles, n_steps):
    i = pl.program_id(1)
    step = pl.program_id(0) * n_tiles + i
    slot = step % 2
    l = t // SUBLANES
    fc = MXU_COLS
    hf8 = (ffn_k - 1) * SUBLANES
    n_rc = t // ROW_CHUNK
    n_chunks = d_ff // fc
    n_mb = t // UP_ROWS

    _tile_prologue(x_hbm, o_hbm, xin, xout, sem_in, sem_out, step, n_steps, n_tiles, t)
    xt = xin.at[slot]
    ot = xout.at[slot]

    @pl.when(i == 0)
    def _():
        gcarry[...] = jnp.zeros_like(gcarry)
        vcarry[...] = jnp.zeros_like(vcarry)

    sh2 = mod_ref[3:4, :]
    gain2 = g2_ref[...] * (1.0 + mod_ref[4:5, :])
    ga2 = mod_ref[5:6, :]

    ups = ((upg0, upv0), (upg1, upv1))
    fbufs = (fbuf0, fbuf1)

    def up_pieces(c):
        def block(which, mb):
            lo = which * d_ff + c * fc
            rows = slice(mb * UP_ROWS, (mb + 1) * UP_ROWS)
            ups[c % 2][which][hf8 + mb * UP_ROWS:hf8 + (mb + 1) * UP_ROWS, :] = _dot(
                hbuf[rows, :], wup_ref[:, lo:lo + fc])
        return [functools.partial(block, which, mb) for which in range(2) for mb in range(n_mb)]

    def conv_rows(buf, r0, cols):
        y = fcw_ref[0:1, cols] * buf[r0:r0 + ROW_CHUNK, :] + fcb_ref[:, cols]
        for k in range(1, ffn_k):
            y = y + fcw_ref[k:k + 1, cols] * buf[r0 + k * SUBLANES:r0 + k * SUBLANES + ROW_CHUNK, :]
        return y

    def gate_pieces(c):
        bg, bv = ups[c % 2]
        gcols = slice(c * fc, (c + 1) * fc)
        vcols = slice(d_ff + c * fc, d_ff + (c + 1) * fc)

        def piece(rc):
            if rc == 0:
                for buf, carry in ((bg, gcarry), (bv, vcarry)):
                    tail = buf[t:hf8 + t, :]
                    buf[0:hf8, :] = _halo_groups(tail, carry[c])
                    carry[c] = tail
            r0 = rc * ROW_CHUNK
            yg = conv_rows(bg, r0, gcols)
            yv = conv_rows(bv, r0, vcols)
            fbufs[c % 2][r0:r0 + ROW_CHUNK, :] = (yg * _sigmoid(yg) * yv).astype(BF16)
        return [functools.partial(piece, rc) for rc in range(n_rc)]

    def down_pieces(c):
        def piece(nq):
            cols = slice(nq * MXU_COLS, (nq + 1) * MXU_COLS)
            down = _dot(fbufs[c % 2][...], wdn_ref[c * fc:(c + 1) * fc, cols])
            if c == 0:
                acc[:, cols] = down
            else:
                acc[:, cols] += down
        return [functools.partial(piece, nq) for nq in range(d // MXU_COLS)]

    ga1 = mod_ref[2:3, :]
    n_lb = d // LANES
    up0 = up_pieces(0)
    gpb = UP_ROWS // SUBLANES
    sub = CHAIN_ROWS // NORM_ROWS
    n_cb = t // CHAIN_ROWS

    def ln_rows(cb):
        for rc in range(sub):
            rows = slice(cb * CHAIN_ROWS + rc * NORM_ROWS, cb * CHAIN_ROWS + (rc + 1) * NORM_ROWS)
            v = jnp.concatenate([cs_ref[n, rows, :] for n in range(n_lb)], axis=-1)
            mu = jnp.mean(v, axis=-1, keepdims=True)
            vc = v - mu
            var = jnp.mean(vc * vc, axis=-1, keepdims=True)
            y = vc * lax.rsqrt(var + EPS) * lng_ref[...] + lnb_ref[...]
            act[rows, :] = (y * _sigmoid(y)).astype(BF16)

    def mix_tail(cb):
        blk = slice(cb * CHAIN_ROWS, (cb + 1) * CHAIN_ROWS)
        ybuf[...] = _dot(act[blk, :], wco_ref[...])
        for rc in range(sub):
            rows = slice(cb * CHAIN_ROWS + rc * NORM_ROWS, cb * CHAIN_ROWS + (rc + 1) * NORM_ROWS)
            gb = jnp.concatenate([sgb_ref[n, rows, :] for n in range(n_lb)], axis=-1)
            mbuf[rows, :] = (gya_ref[rows, :]
                             + gb * ybuf[rc * NORM_ROWS:(rc + 1) * NORM_ROWS, :]).astype(BF16)
        ybuf[...] = _dot(mbuf[blk, :], wout_ref[...])
        for mb in range(cb * CHAIN_ROWS // UP_ROWS, (cb + 1) * CHAIN_ROWS // UP_ROWS):
            for jl in range(gpb):
                j = mb * gpb + jl
                r0 = j * SUBLANES - cb * CHAIN_ROWS
                x1buf[j] = xt[j] + ga1 * ybuf[r0:r0 + SUBLANES, :]
            _norm_modulate_to_bf16(x1buf, hbuf, gain2, sh2, gpb, mb * gpb)
            up0[mb]()
            up0[n_mb + mb]()

    ln_rows(0)
    for cb in range(n_cb):
        if cb + 1 < n_cb:
            ln_rows(cb + 1)
        mix_tail(cb)

    for c in range(n_chunks):
        mxu = up_pieces(c + 1) if c + 1 < n_chunks else []
        mxu += down_pieces(c - 1) if c >= 1 else []
        _interleave(mxu, gate_pieces(c))
    _interleave(down_pieces(n_chunks - 1))

    for j in range(l):
        x2 = x1buf[j] + ga2 * acc[j * SUBLANES:(j + 1) * SUBLANES, :]
        ms = jnp.mean(x2 * x2, axis=-1, keepdims=True)
        ot[j] = x2 * lax.rsqrt(ms + EPS) * gf_ref[...]

    _tile_epilogue(o_hbm, xout, sem_out, step, n_steps, n_tiles, t)


def _ffn_call(x, cs, sgb, gya, mod, lng, lnb, w_co, w_out, g2, w_up, fcw, fcb, w_dn, g_final):
    bsz, seq, d = x.shape
    n_lb = cs.shape[2]
    t = TOKENS_PER_TILE
    l = t // SUBLANES
    n_tiles = seq // t
    d_ff = w_dn.shape[0]
    ffn_k = fcw.shape[0]
    hf8 = (ffn_k - 1) * SUBLANES
    n_chunks = d_ff // MXU_COLS
    kern = functools.partial(_ffn_kernel, d=d, d_ff=d_ff, ffn_k=ffn_k, t=t,
                             n_tiles=n_tiles, n_steps=bsz * n_tiles)
    weights = (lng, lnb, w_co, w_out, g2, w_up, fcw, fcb, w_dn, g_final)
    return pl.pallas_call(
        kern,
        grid=(bsz, n_tiles),
        in_specs=[pl.BlockSpec(memory_space=pl.ANY),
                  _slab_spec(n_lb, t), _slab_spec(n_lb, t),
                  pl.BlockSpec((None, t, d), lambda b, i: (b, i, 0)),
                  pl.BlockSpec((None, N_MOD, d), lambda b, i: (b, 0, 0))]
                 + [_const_spec(w.shape) for w in weights],
        out_specs=pl.BlockSpec(memory_space=pl.ANY),
        out_shape=jax.ShapeDtypeStruct(x.shape, F32),
        scratch_shapes=[
            pltpu.VMEM((2, l, SUBLANES, d), F32),
            pltpu.VMEM((2, l, SUBLANES, d), F32),
            pltpu.SemaphoreType.DMA((2, SUBLANES)),
            pltpu.SemaphoreType.DMA((2, SUBLANES)),
            pltpu.VMEM((t, d), BF16),
            pltpu.VMEM((t, d), BF16),
            pltpu.VMEM((CHAIN_ROWS, d), F32),
            pltpu.VMEM((l, SUBLANES, d), F32),
            pltpu.VMEM((t, d), BF16),
            pltpu.VMEM((hf8 + t, MXU_COLS), F32),
            pltpu.VMEM((hf8 + t, MXU_COLS), F32),
            pltpu.VMEM((hf8 + t, MXU_COLS), F32),
            pltpu.VMEM((hf8 + t, MXU_COLS), F32),
            pltpu.VMEM((t, MXU_COLS), BF16),
            pltpu.VMEM((t, MXU_COLS), BF16),
            pltpu.VMEM((t, d), F32),
            pltpu.VMEM((n_chunks, hf8, MXU_COLS), F32),
            pltpu.VMEM((n_chunks, hf8, MXU_COLS), F32),
        ],
        compiler_params=pltpu.CompilerParams(
            dimension_semantics=("arbitrary", "arbitrary"),
            vmem_limit_bytes=VMEM_LIMIT_BYTES),
        name="conv_ffn",
    )(x, cs, sgb, gya, mod, *weights)


def _slab_cols(a, n_parts):
    r, cols = a.shape
    dd = cols // n_parts
    a4 = a.reshape(r, n_parts, dd // LANES, LANES)
    return jnp.transpose(a4, (2, 0, 1, 3)).reshape(dd // LANES, r, n_parts * LANES)


def kernel(x, c, w_ada, b_ada, g_norm1, w_in, w_pool_grp, b_pool_grp, pool_scale, w_pool_out, conv_w, conv_b, ln_g, ln_b, w_conv_out, w_out, g_norm2, w_up, ffn_conv_w, ffn_conv_b, w_down, g_final):
    bsz, seq, d = x.shape
    depth = w_in.shape[0]
    t = TOKENS_PER_TILE
    d_ff = w_down.shape[1]
    w_pool = w_pool_out.shape[1]
    assert depth == 1, "the final RMSNorm is fused into the (single) ConvFFN call"
    assert seq % t == 0 and d_ff % MXU_COLS == 0 and d % MXU_COLS == 0
    assert t // SUBLANES >= conv_w.shape[1] - 1 and (d // LANES) % 2 == 0 and d // LANES >= 4
    assert all(t % n == 0 for n in (UP_ROWS, ROW_CHUNK, NORM_ROWS))

    for l in range(depth):
        mod = _ada_call(c, w_ada[l], b_ada[l]).reshape(bsz, N_MOD, d)
        w_in_bf = w_in[l].astype(BF16)
        cs, sgb, gya = _mix_call(
            x, mod, g_norm1[l][None], w_in_bf[:, :w_pool], _slab_cols(w_in_bf[:, w_pool:], 4),
            w_pool_grp[l].astype(BF16), b_pool_grp[l].reshape(1, -1), pool_scale[l][None],
            w_pool_out[l].astype(BF16), _slab_cols(conv_w[l], 1), _slab_cols(conv_b[l][None], 1))
        x = _ffn_call(
            x, cs, sgb, gya, mod, ln_g[l][None], ln_b[l][None], w_conv_out[l].astype(BF16),
            w_out[l].astype(BF16), g_norm2[l][None], w_up[l].astype(BF16), ffn_conv_w[l],
            ffn_conv_b[l][None], w_down[l].astype(BF16), g_final[None])
    return x
```

```python
import functools

import jax
import jax.numpy as jnp
from jax import lax
from jax.experimental import pallas as pl
from jax.experimental.pallas import tpu as pltpu

F32 = jnp.float32
BF16 = jnp.bfloat16

EPS = 1e-6
POOL_WINDOWS = (2, 4, 8, 16)
N_MOD = 6

SUBLANES = 8
LANES = 128
MXU_COLS = 256
TOKENS_PER_TILE = 512
ROW_CHUNK = 128
NORM_ROWS = 16
UP_ROWS = 128
CHAIN_ROWS = 512
ADA_COLS = 512
VMEM_LIMIT_BYTES = 60 * 1024 * 1024


def _dot(a, b):
    return jnp.dot(a, b, preferred_element_type=F32)


def _sigmoid(x):
    return jax.nn.sigmoid(x)


def _interleave(*piece_lists):
    keyed = []
    for li, pieces in enumerate(piece_lists):
        for k, piece in enumerate(pieces):
            keyed.append((k / len(pieces), li, k, piece))
    for _, _, _, piece in sorted(keyed, key=lambda e: e[:3]):
        piece()


def _halo_groups(cur_tail, prev_tail):
    n = cur_tail.shape[0] // SUBLANES
    sub = lax.broadcasted_iota(jnp.int32, cur_tail.shape, 0) % SUBLANES
    mixed = jnp.where(sub == SUBLANES - 1, prev_tail, cur_tail)
    groups = [pltpu.roll(mixed[g * SUBLANES:(g + 1) * SUBLANES], 1, 0) for g in range(n)]
    return jnp.concatenate(groups, axis=0) if n > 1 else groups[0]


def _tile_copies(hbm, buf, sem, step, n_tiles, t, to_vmem):
    l = t // SUBLANES
    b = step // n_tiles
    row0 = (step % n_tiles) * t
    slot = step % 2
    copies = []
    for s in range(SUBLANES):
        h = hbm.at[b, pl.ds(row0 + s * l, l), :]
        v = buf.at[slot, :, s, :]
        src, dst = (h, v) if to_vmem else (v, h)
        copies.append(pltpu.make_async_copy(src, dst, sem.at[slot, s]))
    return copies


def _start(copies):
    for k, cp in enumerate(copies):
        cp.start(priority=k % 2)


def _wait(copies):
    for cp in copies:
        cp.wait()


def _tile_in_prologue(x_hbm, xin, sem_in, step, n_steps, n_tiles, t):
    @pl.when(step == 0)
    def _():
        _start(_tile_copies(x_hbm, xin, sem_in, step, n_tiles, t, True))

    @pl.when(step + 1 < n_steps)
    def _():
        _start(_tile_copies(x_hbm, xin, sem_in, step + 1, n_tiles, t, True))

    _wait(_tile_copies(x_hbm, xin, sem_in, step, n_tiles, t, True))


def _tile_prologue(x_hbm, o_hbm, xin, xout, sem_in, sem_out, step, n_steps, n_tiles, t):
    _tile_in_prologue(x_hbm, xin, sem_in, step, n_steps, n_tiles, t)

    @pl.when(step >= 2)
    def _():
        _wait(_tile_copies(o_hbm, xout, sem_out, step - 2, n_tiles, t, False))


def _tile_epilogue(o_hbm, xout, sem_out, step, n_steps, n_tiles, t):
    _start(_tile_copies(o_hbm, xout, sem_out, step, n_tiles, t, False))

    @pl.when(step == n_steps - 1)
    def _():
        if n_steps >= 2:
            _wait(_tile_copies(o_hbm, xout, sem_out, step - 1, n_tiles, t, False))
        _wait(_tile_copies(o_hbm, xout, sem_out, step, n_tiles, t, False))


def _norm_modulate_to_bf16(xt, hbuf, gain, shift, n_groups, first_group=0):
    for jp in range(first_group // 2, (first_group + n_groups) // 2):
        xx = jnp.concatenate([xt[2 * jp], xt[2 * jp + 1]], axis=0)
        ms = jnp.mean(xx * xx, axis=-1, keepdims=True)
        h = xx * lax.rsqrt(ms + EPS) * gain + shift
        hbuf[jp * 16:(jp + 1) * 16, :] = h.astype(BF16)


def _ada_kernel(c_ref, w_ref, b_ref, o_ref):
    c = c_ref[...]
    ca = c * _sigmoid(c)
    w = w_ref[...]
    ca_hi = ca.astype(BF16)
    ca_lo = (ca - ca_hi.astype(F32)).astype(BF16)
    w_hi = w.astype(BF16)
    w_lo = (w - w_hi.astype(F32)).astype(BF16)
    acc = _dot(ca_hi, w_hi) + _dot(ca_lo, w_hi) + _dot(ca_hi, w_lo)
    o_ref[...] = acc + b_ref[...]


def _ada_call(c, w_ada, b_ada):
    b, d = c.shape
    n = w_ada.shape[1]
    return pl.pallas_call(
        _ada_kernel,
        grid=(n // ADA_COLS,),
        in_specs=[
            pl.BlockSpec((b, d), lambda j: (0, 0)),
            pl.BlockSpec((d, ADA_COLS), lambda j: (0, j)),
            pl.BlockSpec((1, ADA_COLS), lambda j: (0, j)),
        ],
        out_specs=pl.BlockSpec((b, ADA_COLS), lambda j: (0, j)),
        out_shape=jax.ShapeDtypeStruct((b, n), F32),
        compiler_params=pltpu.CompilerParams(dimension_semantics=("arbitrary",)),
        name="adaln_mod",
    )(c, w_ada, b_ada.reshape(1, n))


def _mix_kernel(x_hbm, mod_ref, g1_ref, wzp_ref, wslab_ref, wgrp_ref, bgrp_ref, pscale_ref,
                wpo_ref, cw_ref, cb_ref,
                cslab, sgb, gya_ref,
                xin, sem_in,
                hbuf, zpbuf, zraw0, zraw1, ubuf0, ubuf1, sga, invc, pbuf, ya,
                zcarry, ucarry,
                *, d, w_pool, conv_k, t, n_tiles, n_steps):
    i = pl.program_id(1)
    step = pl.program_id(0) * n_tiles + i
    slot = step % 2
    l = t // SUBLANES
    n_grp = len(POOL_WINDOWS)
    pg = w_pool // n_grp
    hp8 = max(POOL_WINDOWS) * SUBLANES
    hu8 = (conv_k - 1) * SUBLANES
    n_rc = t // ROW_CHUNK
    n_lb = d // LANES

    _tile_in_prologue(x_hbm, xin, sem_in, step, n_steps, n_tiles, t)
    xt = xin.at[slot]

    @pl.when(i == 0)
    def _():
        zcarry[...] = jnp.zeros_like(zcarry)
        ucarry[...] = jnp.zeros_like(ucarry)
        r = lax.broadcasted_iota(jnp.int32, (t, pg), 0)
        tok = (r % SUBLANES) * l + r // SUBLANES
        for g, w in enumerate(POOL_WINDOWS):
            cnt = jnp.minimum(tok + 1, w).astype(F32)
            invc[:, g * pg:(g + 1) * pg] = 1.0 / cnt

    @pl.when(i == 1)
    def _():
        for g, w in enumerate(POOL_WINDOWS):
            invc[:, g * pg:(g + 1) * pg] = jnp.full((t, pg), 1.0 / w, F32)

    sh1 = mod_ref[0:1, :]
    gain1 = g1_ref[...] * (1.0 + mod_ref[1:2, :])

    def dot_pieces(n, zraw):
        def piece():
            zraw[...] = _dot(hbuf[...], wslab_ref[n])
        return [piece]

    def glu_pieces(n, zraw, ubuf):
        def piece(rc):
            r0 = rc * ROW_CHUNK
            blk = zraw[r0:r0 + ROW_CHUNK, :]
            ubuf[hu8 + r0:hu8 + r0 + ROW_CHUNK, :] = (
                blk[:, 0:LANES] * _sigmoid(blk[:, LANES:2 * LANES]))
            sga[n, r0:r0 + ROW_CHUNK, :] = _sigmoid(blk[:, 2 * LANES:3 * LANES])
            sgb[n, r0:r0 + ROW_CHUNK, :] = _sigmoid(blk[:, 3 * LANES:4 * LANES])
        return [functools.partial(piece, rc) for rc in range(n_rc)]

    def conv_pieces(n, ubuf):
        def piece(rc):
            if rc == 0:
                tail = ubuf[t:hu8 + t, :]
                ubuf[0:hu8, :] = _halo_groups(tail, ucarry[n])
                ucarry[n] = tail
            r0 = rc * ROW_CHUNK
            n_out = ROW_CHUNK // SUBLANES
            win = [ubuf[r0 + g * SUBLANES:r0 + (g + 1) * SUBLANES, :]
                   for g in range(n_out + conv_k - 1)]
            bias = jnp.broadcast_to(cb_ref[n], (SUBLANES, LANES))
            accs = [bias] * n_out
            for k0 in range(0, conv_k, 4):
                ks = range(k0, min(k0 + 4, conv_k))
                wks = [jnp.broadcast_to(cw_ref[n, k:k + 1, :], (SUBLANES, LANES)) for k in ks]
                for j in range(n_out):
                    prods = [wk * win[j + k] for wk, k in zip(wks, ks)]
                    while len(prods) > 1:
                        prods = [a + b for a, b in zip(prods[::2], prods[1::2])] + (
                            [prods[-1]] if len(prods) % 2 else [])
                    accs[j] = accs[j] + prods[0]
            cslab[n, r0:r0 + ROW_CHUNK, :] = jnp.concatenate(accs, axis=0)
        return [functools.partial(piece, rc) for rc in range(n_rc)]

    def pool_stage():
        z_tail = zpbuf[t:hp8 + t, :]
        zpbuf[0:hp8, :] = _halo_groups(z_tail, zcarry[...])
        zcarry[...] = z_tail
        for rc in range(n_rc):
            r0 = rc * ROW_CHUNK
            for g, w in enumerate(POOL_WINDOWS):
                cols = slice(g * pg, (g + 1) * pg)
                zc = zpbuf[hp8 + r0:hp8 + r0 + ROW_CHUNK, cols]
                s = zc
                for k in range(1, w):
                    s = s + zpbuf[hp8 + r0 - k * SUBLANES:hp8 + r0 - k * SUBLANES + ROW_CHUNK, cols]
                pooled = s * invc[r0:r0 + ROW_CHUNK, cols] - zc
                pbuf[r0:r0 + ROW_CHUNK, cols] = pooled.astype(BF16)

    def zp_piece():
        zpbuf[hp8:hp8 + t, :] = _dot(hbuf[...], wzp_ref[...])

    def group_piece(g):
        cols = slice(g * pg, (g + 1) * pg)
        pa = _dot(pbuf[:, cols], wgrp_ref[g]) + bgrp_ref[:, cols]
        pbuf[:, cols] = (pa * pscale_ref[:, cols]).astype(BF16)

    def ya_piece():
        ya[...] = _dot(pbuf[...], wpo_ref[...])

    _norm_modulate_to_bf16(xt, hbuf, gain1, sh1, l)
    zp_piece()
    _interleave(dot_pieces(0, zraw0), [pool_stage])
    _interleave(dot_pieces(1, zraw1), glu_pieces(0, zraw0, ubuf0))

    def slab_pair_step(m, carry):
        n = 2 * m
        _interleave(dot_pieces(n, zraw0), conv_pieces(n - 2, ubuf0),
                    glu_pieces(n - 1, zraw1, ubuf1))
        _interleave(dot_pieces(n + 1, zraw1), conv_pieces(n - 1, ubuf1),
                    glu_pieces(n, zraw0, ubuf0))
        return carry
    lax.fori_loop(1, n_lb // 2, slab_pair_step, 0)

    _interleave([functools.partial(group_piece, g) for g in range(n_grp)],
                conv_pieces(n_lb - 2, ubuf0), glu_pieces(n_lb - 1, zraw1, ubuf1))
    _interleave([ya_piece], conv_pieces(n_lb - 1, ubuf1))

    for rc in range(t // NORM_ROWS):
        rows = slice(rc * NORM_ROWS, (rc + 1) * NORM_ROWS)
        ga = jnp.concatenate([sga[n, rows, :] for n in range(n_lb)], axis=-1)
        gya_ref[rows, :] = ga * ya[rows, :]


def _const_spec(shape):
    nd = len(shape)
    return pl.BlockSpec(shape, lambda b, i: (0,) * nd, pipeline_mode=pl.Buffered(1))


def _slab_spec(n_lb, t):
    return pl.BlockSpec((None, None, n_lb, t, LANES), lambda b, i: (b, i, 0, 0, 0))


def _mix_call(x, mod, g1, w_zp, w_slab, w_grp, b_grp, pscale, w_po, cw, cb):
    bsz, seq, d = x.shape
    t = TOKENS_PER_TILE
    l = t // SUBLANES
    n_tiles = seq // t
    w_pool = w_po.shape[0]
    n_lb, conv_k, _ = cw.shape
    hp8 = max(POOL_WINDOWS) * SUBLANES
    hu8 = (conv_k - 1) * SUBLANES
    kern = functools.partial(_mix_kernel, d=d, w_pool=w_pool, conv_k=conv_k, t=t,
                             n_tiles=n_tiles, n_steps=bsz * n_tiles)
    weights = (g1, w_zp, w_slab, w_grp, b_grp, pscale, w_po, cw, cb)
    slab_shape = jax.ShapeDtypeStruct((bsz, n_tiles, n_lb, t, LANES), F32)
    return pl.pallas_call(
        kern,
        grid=(bsz, n_tiles),
        in_specs=[pl.BlockSpec(memory_space=pl.ANY),
                  pl.BlockSpec((None, N_MOD, d), lambda b, i: (b, 0, 0))]
                 + [_const_spec(w.shape) for w in weights],
        out_specs=[_slab_spec(n_lb, t), _slab_spec(n_lb, t),
                   pl.BlockSpec((None, t, d), lambda b, i: (b, i, 0))],
        out_shape=[slab_shape, slab_shape, jax.ShapeDtypeStruct(x.shape, F32)],
        scratch_shapes=[
            pltpu.VMEM((2, l, SUBLANES, d), F32),
            pltpu.SemaphoreType.DMA((2, SUBLANES)),
            pltpu.VMEM((t, d), BF16),
            pltpu.VMEM((hp8 + t, w_pool), F32),
            pltpu.VMEM((t, 4 * LANES), F32),
            pltpu.VMEM((t, 4 * LANES), F32),
            pltpu.VMEM((hu8 + t, LANES), F32),
            pltpu.VMEM((hu8 + t, LANES), F32),
            pltpu.VMEM((n_lb, t, LANES), F32),
            pltpu.VMEM((t, w_pool), F32),
            pltpu.VMEM((t, w_pool), BF16),
            pltpu.VMEM((t, d), F32),
            pltpu.VMEM((hp8, w_pool), F32),
            pltpu.VMEM((n_lb, hu8, LANES), F32),
        ],
        compiler_params=pltpu.CompilerParams(
            dimension_semantics=("arbitrary", "arbitrary"),
            vmem_limit_bytes=VMEM_LIMIT_BYTES),
        name="token_mix",
    )(x, mod, *weights)


def _ffn_kernel(x_hbm, cs_ref, sgb_ref, gya_ref, mod_ref, lng_ref, lnb_ref, wco_ref, wout_ref,
                g2_ref, wup_ref, fcw_ref, fcb_ref, wdn_ref, gf_ref,
                o_hbm,
                xin, xout, sem_in, sem_out,
                act, mbuf, ybuf, x1buf, hbuf, upg0, upv0, upg1, upv1, fbuf0, fbuf1, acc, gcarry, vcarry,
                *, d, d_ff, ffn_k, t, n_tiles, n_steps):
    i = pl.program_id(1)
    step = pl.program_id(0) * n_tiles + i
    slot = step % 2
    l = t // SUBLANES
    fc = MXU_COLS
    hf8 = (ffn_k - 1) * SUBLANES
    n_rc = t // ROW_CHUNK
    n_chunks = d_ff // fc
    n_mb = t // UP_ROWS

    _tile_prologue(x_hbm, o_hbm, xin, xout, sem_in, sem_out, step, n_steps, n_tiles, t)
    xt = xin.at[slot]
    ot = xout.at[slot]

    @pl.when(i == 0)
    def _():
        gcarry[...] = jnp.zeros_like(gcarry)
        vcarry[...] = jnp.zeros_like(vcarry)

    sh2 = mod_ref[3:4, :]
    gain2 = g2_ref[...] * (1.0 + mod_ref[4:5, :])
    ga2 = mod_ref[5:6, :]

    ups = ((upg0, upv0), (upg1, upv1))
    fbufs = (fbuf0, fbuf1)

    def up_pieces(c):
        def block(which, mb):
            lo = which * d_ff + c * fc
            rows = slice(mb * UP_ROWS, (mb + 1) * UP_ROWS)
            ups[c % 2][which][hf8 + mb * UP_ROWS:hf8 + (mb + 1) * UP_ROWS, :] = _dot(
                hbuf[rows, :], wup_ref[:, lo:lo + fc])
        return [functools.partial(block, which, mb) for which in range(2) for mb in range(n_mb)]

    def conv_rows(buf, r0, cols):
        y = fcw_ref[0:1, cols] * buf[r0:r0 + ROW_CHUNK, :] + fcb_ref[:, cols]
        for k in range(1, ffn_k):
            y = y + fcw_ref[k:k + 1, cols] * buf[r0 + k * SUBLANES:r0 + k * SUBLANES + ROW_CHUNK, :]
        return y

    def gate_pieces(c):
        bg, bv = ups[c % 2]
        gcols = slice(c * fc, (c + 1) * fc)
        vcols = slice(d_ff + c * fc, d_ff + (c + 1) * fc)

        def piece(rc):
            if rc == 0:
                for buf, carry in ((bg, gcarry), (bv, vcarry)):
                    tail = buf[t:hf8 + t, :]
                    buf[0:hf8, :] = _halo_groups(tail, carry[c])
                    carry[c] = tail
            r0 = rc * ROW_CHUNK
            yg = conv_rows(bg, r0, gcols)
            yv = conv_rows(bv, r0, vcols)
            fbufs[c % 2][r0:r0 + ROW_CHUNK, :] = (yg * _sigmoid(yg) * yv).astype(BF16)
        return [functools.partial(piece, rc) for rc in range(n_rc)]

    def down_pieces(c):
        def piece(nq):
            cols = slice(nq * MXU_COLS, (nq + 1) * MXU_COLS)
            down = _dot(fbufs[c % 2][...], wdn_ref[c * fc:(c + 1) * fc, cols])
            if c == 0:
                acc[:, cols] = down
            else:
                acc[:, cols] += down
        return [functools.partial(piece, nq) for nq in range(d // MXU_COLS)]

    ga1 = mod_ref[2:3, :]
    n_lb = d // LANES
    up0 = up_pieces(0)
    gpb = UP_ROWS // SUBLANES
    sub = CHAIN_ROWS // NORM_ROWS
    n_cb = t // CHAIN_ROWS

    def ln_rows(cb):
        for rc in range(sub):
            rows = slice(cb * CHAIN_ROWS + rc * NORM_ROWS, cb * CHAIN_ROWS + (rc + 1) * NORM_ROWS)
            v = jnp.concatenate([cs_ref[n, rows, :] for n in range(n_lb)], axis=-1)
            mu = jnp.mean(v, axis=-1, keepdims=True)
            vc = v - mu
            var = jnp.mean(vc * vc, axis=-1, keepdims=True)
            y = vc * lax.rsqrt(var + EPS) * lng_ref[...] + lnb_ref[...]
            act[rows, :] = (y * _sigmoid(y)).astype(BF16)

    def mix_tail(cb):
        blk = slice(cb * CHAIN_ROWS, (cb + 1) * CHAIN_ROWS)
        ybuf[...] = _dot(act[blk, :], wco_ref[...])
        for rc in range(sub):
            rows = slice(cb * CHAIN_ROWS + rc * NORM_ROWS, cb * CHAIN_ROWS + (rc + 1) * NORM_ROWS)
            gb = jnp.concatenate([sgb_ref[n, rows, :] for n in range(n_lb)], axis=-1)
            mbuf[rows, :] = (gya_ref[rows, :]
                             + gb * ybuf[rc * NORM_ROWS:(rc + 1) * NORM_ROWS, :]).astype(BF16)
        ybuf[...] = _dot(mbuf[blk, :], wout_ref[...])
        for mb in range(cb * CHAIN_ROWS // UP_ROWS, (cb + 1) * CHAIN_ROWS // UP_ROWS):
            for jl in range(gpb):
                j = mb * gpb + jl
                r0 = j * SUBLANES - cb * CHAIN_ROWS
                x1buf[j] = xt[j] + ga1 * ybuf[r0:r0 + SUBLANES, :]
            _norm_modulate_to_bf16(x1buf, hbuf, gain2, sh2, gpb, mb * gpb)
            up0[mb]()
            up0[n_mb + mb]()

    ln_rows(0)
    for cb in range(n_cb):
        if cb + 1 < n_cb:
            ln_rows(cb + 1)
        mix_tail(cb)

    for c in range(n_chunks):
        mxu = up_pieces(c + 1) if c + 1 < n_chunks else []
        mxu += down_pieces(c - 1) if c >= 1 else []
        _interleave(mxu, gate_pieces(c))
    _interleave(down_pieces(n_chunks - 1))

    for j in range(l):
        x2 = x1buf[j] + ga2 * acc[j * SUBLANES:(j + 1) * SUBLANES, :]
        ms = jnp.mean(x2 * x2, axis=-1, keepdims=True)
        ot[j] = x2 * lax.rsqrt(ms + EPS) * gf_ref[...]

    _tile_epilogue(o_hbm, xout, sem_out, step, n_steps, n_tiles, t)


def _ffn_call(x, cs, sgb, gya, mod, lng, lnb, w_co, w_out, g2, w_up, fcw, fcb, w_dn, g_final):
    bsz, seq, d = x.shape
    n_lb = cs.shape[2]
    t = TOKENS_PER_TILE
    l = t // SUBLANES
    n_tiles = seq // t
    d_ff = w_dn.shape[0]
    ffn_k = fcw.shape[0]
    hf8 = (ffn_k - 1) * SUBLANES
    n_chunks = d_ff // MXU_COLS
    kern = functools.partial(_ffn_kernel, d=d, d_ff=d_ff, ffn_k=ffn_k, t=t,
                             n_tiles=n_tiles, n_steps=bsz * n_tiles)
    weights = (lng, lnb, w_co, w_out, g2, w_up, fcw, fcb, w_dn, g_final)
    return pl.pallas_call(
        kern,
        grid=(bsz, n_tiles),
        in_specs=[pl.BlockSpec(memory_space=pl.ANY),
                  _slab_spec(n_lb, t), _slab_spec(n_lb, t),
                  pl.BlockSpec((None, t, d), lambda b, i: (b, i, 0)),
                  pl.BlockSpec((None, N_MOD, d), lambda b, i: (b, 0, 0))]
                 + [_const_spec(w.shape) for w in weights],
        out_specs=pl.BlockSpec(memory_space=pl.ANY),
        out_shape=jax.ShapeDtypeStruct(x.shape, F32),
        scratch_shapes=[
            pltpu.VMEM((2, l, SUBLANES, d), F32),
            pltpu.VMEM((2, l, SUBLANES, d), F32),
            pltpu.SemaphoreType.DMA((2, SUBLANES)),
            pltpu.SemaphoreType.DMA((2, SUBLANES)),
            pltpu.VMEM((t, d), BF16),
            pltpu.VMEM((t, d), BF16),
            pltpu.VMEM((CHAIN_ROWS, d), F32),
            pltpu.VMEM((l, SUBLANES, d), F32),
            pltpu.VMEM((t, d), BF16),
            pltpu.VMEM((hf8 + t, MXU_COLS), F32),
            pltpu.VMEM((hf8 + t, MXU_COLS), F32),
            pltpu.VMEM((hf8 + t, MXU_COLS), F32),
            pltpu.VMEM((hf8 + t, MXU_COLS), F32),
            pltpu.VMEM((t, MXU_COLS), BF16),
            pltpu.VMEM((t, MXU_COLS), BF16),
            pltpu.VMEM((t, d), F32),
            pltpu.VMEM((n_chunks, hf8, MXU_COLS), F32),
            pltpu.VMEM((n_chunks, hf8, MXU_COLS), F32),
        ],
        compiler_params=pltpu.CompilerParams(
            dimension_semantics=("arbitrary", "arbitrary"),
            vmem_limit_bytes=VMEM_LIMIT_BYTES),
        name="conv_ffn",
    )(x, cs, sgb, gya, mod, *weights)


def _slab_cols(a, n_parts):
    r, cols = a.shape
    dd = cols // n_parts
    a4 = a.reshape(r, n_parts, dd // LANES, LANES)
    return jnp.transpose(a4, (2, 0, 1, 3)).reshape(dd // LANES, r, n_parts * LANES)


def kernel(x, c, w_ada, b_ada, g_norm1, w_in, w_pool_grp, b_pool_grp, pool_scale, w_pool_out, conv_w, conv_b, ln_g, ln_b, w_conv_out, w_out, g_norm2, w_up, ffn_conv_w, ffn_conv_b, w_down, g_final):
    bsz, seq, d = x.shape
    depth = w_in.shape[0]
    t = TOKENS_PER_TILE
    d_ff = w_down.shape[1]
    w_pool = w_pool_out.shape[1]
    assert depth == 1, "the final RMSNorm is fused into the (single) ConvFFN call"
    assert seq % t == 0 and d_ff % MXU_COLS == 0 and d % MXU_COLS == 0
    assert t // SUBLANES >= conv_w.shape[1] - 1 and (d // LANES) % 2 == 0 and d // LANES >= 4
    assert all(t % n == 0 for n in (UP_ROWS, ROW_CHUNK, NORM_ROWS))

    for l in range(depth):
        mod = _ada_call(c, w_ada[l], b_ada[l]).reshape(bsz, N_MOD, d)
        w_in_bf = w_in[l].astype(BF16)
        cs, sgb, gya = _mix_call(
            x, mod, g_norm1[l][None], w_in_bf[:, :w_pool], _slab_cols(w_in_bf[:, w_pool:], 4),
            w_pool_grp[l].astype(BF16), b_pool_grp[l].reshape(1, -1), pool_scale[l][None],
            w_pool_out[l].astype(BF16), _slab_cols(conv_w[l], 1), _slab_cols(conv_b[l][None], 1))
        x = _ffn_call(
            x, cs, sgb, gya, mod, ln_g[l][None], ln_b[l][None], w_conv_out[l].astype(BF16),
            w_out[l].astype(BF16), g_norm2[l][None], w_up[l].astype(BF16), ffn_conv_w[l],
            ffn_conv_b[l][None], w_down[l].astype(BF16), g_final[None])
    return x
```
